```python
import math, functools
import jax, jax.numpy as jnp
from jax import lax
import numpy as np

D_MODEL = 4096
BATCH = 2
SEQ = 4096
DEPTH = 1
DEC_BATCH = 32
DEC_SEQ = 4
PAST_LEN = 8192
PAGE_SIZE = 128

N_HEADS = 16
N_KV_HEADS = 8
GROUP = N_HEADS // N_KV_HEADS
HEAD_DIM = 128
V_DIM = 2 * HEAD_DIM
ATT_Q_W = N_HEADS * 2 * HEAD_DIM
ATT_K_W = N_KV_HEADS * 2 * HEAD_DIM
ATT_V_W = N_KV_HEADS * V_DIM
ATT_OUT_W = N_HEADS * V_DIM
ROT_DIM = HEAD_DIM // 4
ROPE_THETA = 500000.0
Q_BLOCK = 128
LRU_W = 2048
LRU_HEADS = 16
LRU_BLOCK = LRU_W // LRU_HEADS
LRU_CONV = 4
LRU_C = 8.0
D_FF = 11008
FFN_CONV = 3
IN_W = ATT_Q_W + ATT_K_W + ATT_V_W + 2 * LRU_W + 2 * D_MODEL
DN_ALPHA = (2.0 * DEPTH) ** 0.25
DN_BETA = (8.0 * DEPTH) ** -0.25
NORM_EPS = 1e-5

kernel_name = 'hybrid_diffattn_rglru_convffn_step'


def layer_norm(x, g, b):
    xf = x.astype(jnp.float32)
    mu = jnp.mean(xf, -1, keepdims=True)
    var = jnp.mean(jnp.square(xf - mu), -1, keepdims=True)
    return ((xf - mu) * lax.rsqrt(var + NORM_EPS) * g + b).astype(x.dtype)


def rms_norm(x, g):
    xf = x.astype(jnp.float32)
    ms = jnp.mean(jnp.square(xf), -1, keepdims=True)
    return (xf * lax.rsqrt(ms + NORM_EPS) * g).astype(x.dtype)


def partial_rope(x, pos):
    half = ROT_DIM // 2
    inv_freq = jnp.power(ROPE_THETA, -jnp.arange(half, dtype=jnp.float32) * (2.0 / ROT_DIM))
    ang = pos.astype(jnp.float32)[:, None] * inv_freq[None, :]
    cos = jnp.cos(ang)[None, :, None, :].astype(x.dtype)
    sin = jnp.sin(ang)[None, :, None, :].astype(x.dtype)
    x1 = x[..., :half]
    x2 = x[..., half:ROT_DIM]
    return jnp.concatenate([x1 * cos - x2 * sin, x2 * cos + x1 * sin, x[..., ROT_DIM:]], axis=-1)


def causal_dwconv(buf, x, w, b):
    K = w.shape[0]
    T = x.shape[1]
    full = jnp.concatenate([buf, x], axis=1)
    y = full[:, 0:T] * w[0]
    for j in range(1, K):
        y = y + full[:, j:j + T] * w[j]
    return y + b, full[:, full.shape[1] - (K - 1):]


def rg_lru(xc, h0, gate_w, gate_b, lam):
    B, T, _ = xc.shape
    xh = xc.reshape(B, T, LRU_HEADS, LRU_BLOCK)
    gl = jnp.einsum('bthi,ghij->gbthj', xh, gate_w) + gate_b[:, None, None]
    gates = jax.nn.sigmoid(gl.astype(jnp.float32)).reshape(2, B, T, LRU_W)
    i_gate, r_gate = gates[0], gates[1]
    log_a = -LRU_C * r_gate * jax.nn.softplus(-lam.astype(jnp.float32))
    a = jnp.exp(log_a)
    u = jnp.sqrt(-jnp.expm1(2.0 * log_a)) * (i_gate * xc.astype(jnp.float32))

    def step(h, au):
        a_t, u_t = au
        h = a_t * h + u_t
        return h, h

    hT, hs = lax.scan(step, h0.astype(jnp.float32), (a.swapaxes(0, 1), u.swapaxes(0, 1)))
    return hs.swapaxes(0, 1).astype(xc.dtype), hT.astype(h0.dtype)


def diff_attn_core(q, k, v, q_pos, k_pos, lam):
    s = jnp.einsum('bqhgcd,bkhcd->bhgcqk', q, k).astype(jnp.float32) * (HEAD_DIM ** -0.5)
    mask = k_pos[None, :] <= q_pos[:, None]
    s = jnp.where(mask, s, -jnp.inf)
    p = jax.nn.softmax(s, axis=-1)
    attn = p[:, :, :, 0] - lam * p[:, :, :, 1]
    return jnp.einsum('bhgqk,bkhe->bqhge', attn.astype(v.dtype), v)


def prompt_attention(q, k, v, lam):
    B, T = q.shape[0], q.shape[1]
    nb = T // Q_BLOCK
    qb = q.reshape((B, nb, Q_BLOCK) + q.shape[2:]).swapaxes(0, 1)
    k_pos = jnp.arange(T, dtype=jnp.int32)

    def one(args):
        blk, q_blk = args
        q_pos = blk * Q_BLOCK + jnp.arange(Q_BLOCK, dtype=jnp.int32)
        return diff_attn_core(q_blk, k, v, q_pos, k_pos, lam)

    o = lax.map(one, (jnp.arange(nb, dtype=jnp.int32), qb))
    return o.swapaxes(0, 1).reshape(B, T, N_KV_HEADS, GROUP, V_DIM)


def sample_attention(q, k, v, lam, cache_k, cache_v, page_table, layer):
    T = q.shape[1]
    past = page_table.shape[1] * cache_k.shape[2]
    q_pos = past + jnp.arange(T, dtype=jnp.int32)
    k_pos = jnp.arange(past + T, dtype=jnp.int32)

    def one(args):
        q_b, k_b, v_b, pt_b = args
        k_past = cache_k[layer, pt_b].reshape(past, N_KV_HEADS, 2, HEAD_DIM)
        v_past = cache_v[layer, pt_b].reshape(past, N_KV_HEADS, V_DIM)
        k_all = jnp.concatenate([k_past.astype(k_b.dtype), k_b], axis=0)[None]
        v_all = jnp.concatenate([v_past.astype(v_b.dtype), v_b], axis=0)[None]
        return diff_attn_core(q_b[None], k_all, v_all, q_pos, k_pos, lam)[0]

    return lax.map(one, (q, k, v, page_table))


def trunk_layer(x, positions, attend, h0, lru_buf, ffn_buf, p, lam_init):
    B, T, _ = x.shape
    proj = jnp.einsum('btd,de->bte', x, p['w_in'])
    o1 = ATT_Q_W
    o2 = o1 + ATT_K_W
    o3 = o2 + ATT_V_W
    o4 = o3 + LRU_W
    o5 = o4 + LRU_W
    q, k, v, xb, yb, gate_logits = jnp.split(proj, [o1, o2, o3, o4, o5], axis=-1)
    q = partial_rope(q.reshape(B, T, 2 * N_HEADS, HEAD_DIM), positions).reshape(B, T, N_KV_HEADS, GROUP, 2, HEAD_DIM)
    k = partial_rope(k.reshape(B, T, 2 * N_KV_HEADS, HEAD_DIM), positions).reshape(B, T, N_KV_HEADS, 2, HEAD_DIM)
    v = v.reshape(B, T, N_KV_HEADS, V_DIM)
    lv = p['diff_lambda'].astype(jnp.float32)
    lam = jnp.exp(jnp.sum(lv[0] * lv[1])) - jnp.exp(jnp.sum(lv[2] * lv[3])) + lam_init
    o_att = attend(q, k, v, lam)
    o_att = (rms_norm(o_att, p['subln_w']) * (1.0 - lam_init)).reshape(B, T, ATT_OUT_W)
    xc, new_lru_buf = causal_dwconv(lru_buf, xb, p['lru_conv_w'], p['lru_conv_b'])
    hs, hT = rg_lru(xc, h0, p['lru_gate_w'], p['lru_gate_b'], p['lru_lambda'])
    o_lru = hs * jax.nn.gelu(yb)
    g = jax.nn.sigmoid(gate_logits.astype(jnp.float32)).astype(x.dtype)
    wb = p['w_branch']
    merged = g[..., :D_MODEL] * (o_att @ wb[:ATT_OUT_W]) + g[..., D_MODEL:] * (o_lru @ wb[ATT_OUT_W:])
    x = layer_norm(DN_ALPHA * x + merged @ p['w_out'], p['ln1_g'], p['ln1_b'])
    u, up = jnp.split(x @ p['ffn_w_in'], [D_FF], axis=-1)
    uc, new_ffn_buf = causal_dwconv(ffn_buf, u, p['ffn_conv_w'], p['ffn_conv_b'])
    f = (jax.nn.silu(uc) * up) @ p['ffn_w_down']
    x = layer_norm(DN_ALPHA * x + f, p['ln2_g'], p['ln2_b'])
    return x, k, v, hT, new_lru_buf, new_ffn_buf


def setup_inputs(seed: int = 0) -> dict:
    key = jax.random.key(seed)
    ks = jax.random.split(key, 32)
    f32 = jnp.float32
    n_pages = PAST_LEN // PAGE_SIZE
    n_used = DEC_BATCH * n_pages
    n_pool = n_used + max(1, n_used // 4)

    def nrm(k, shape, s):
        return jax.random.normal(k, shape, f32) * s

    r = jax.random.uniform(ks[15], (DEPTH, LRU_W), f32, 0.9, 0.999)
    sg = r ** (1.0 / LRU_C)
    return {
        'x_prompt': nrm(ks[0], (BATCH, SEQ, D_MODEL), 1.0),
        'x_sample': nrm(ks[1], (DEC_BATCH, DEC_SEQ, D_MODEL), 1.0),
        'cache_k': nrm(ks[2], (DEPTH, n_pool, PAGE_SIZE, N_KV_HEADS, 2, HEAD_DIM), 1.0),
        'cache_v': nrm(ks[3], (DEPTH, n_pool, PAGE_SIZE, N_KV_HEADS, V_DIM), 1.0),
        'page_table': jax.random.permutation(ks[4], n_pool)[:n_used].reshape(DEC_BATCH, n_pages).astype(jnp.int32),
        'state_lru_h': nrm(ks[5], (DEPTH, DEC_BATCH, LRU_W), 0.5),
        'state_lru_conv': nrm(ks[6], (DEPTH, DEC_BATCH, LRU_CONV - 1, LRU_W), 1.0),
        'state_ffn_conv': nrm(ks[7], (DEPTH, DEC_BATCH, FFN_CONV - 1, D_FF), 1.0),
        'w_in': nrm(ks[8], (DEPTH, D_MODEL, IN_W), D_MODEL ** -0.5),
        'diff_lambda': nrm(ks[9], (DEPTH, 4, HEAD_DIM), 0.1),
        'subln_w': 1.0 + nrm(ks[10], (DEPTH, V_DIM), 0.02),
        'lru_conv_w': nrm(ks[11], (DEPTH, LRU_CONV, LRU_W), LRU_CONV ** -0.5),
        'lru_conv_b': nrm(ks[12], (DEPTH, LRU_W), 0.02),
        'lru_gate_w': nrm(ks[13], (DEPTH, 2, LRU_HEADS, LRU_BLOCK, LRU_BLOCK), LRU_BLOCK ** -0.5),
        'lru_gate_b': nrm(ks[14], (DEPTH, 2, LRU_HEADS, LRU_BLOCK), 0.02),
        'lru_lambda': jnp.log(sg) - jnp.log1p(-sg),
        'w_branch': nrm(ks[16], (DEPTH, ATT_OUT_W + LRU_W, D_MODEL), DN_BETA * (ATT_OUT_W + LRU_W) ** -0.5),
        'w_out': nrm(ks[17], (DEPTH, D_MODEL, D_MODEL), DN_BETA * D_MODEL ** -0.5),
        'ln1_g': 1.0 + nrm(ks[18], (DEPTH, D_MODEL), 0.02),
        'ln1_b': nrm(ks[19], (DEPTH, D_MODEL), 0.02),
        'ffn_w_in': nrm(ks[20], (DEPTH, D_MODEL, 2 * D_FF), D_MODEL ** -0.5),
        'ffn_conv_w': nrm(ks[21], (DEPTH, FFN_CONV, D_FF), FFN_CONV ** -0.5),
        'ffn_conv_b': nrm(ks[22], (DEPTH, D_FF), 0.02),
        'ffn_w_down': nrm(ks[23], (DEPTH, D_FF, D_MODEL), DN_BETA * D_FF ** -0.5),
        'ln2_g': 1.0 + nrm(ks[24], (DEPTH, D_MODEL), 0.02),
        'ln2_b': nrm(ks[25], (DEPTH, D_MODEL), 0.02),
    }


def reference(x_prompt, x_sample, cache_k, cache_v, page_table, state_lru_h, state_lru_conv, state_ffn_conv,
              w_in, diff_lambda, subln_w, lru_conv_w, lru_conv_b, lru_gate_w, lru_gate_b, lru_lambda,
              w_branch, w_out, ln1_g, ln1_b, ffn_w_in, ffn_conv_w, ffn_conv_b, ffn_w_down, ln2_g, ln2_b):
    bp, tp = x_prompt.shape[0], x_prompt.shape[1]
    ts = x_sample.shape[1]
    past = page_table.shape[1] * cache_k.shape[2]
    pos_p = jnp.arange(tp, dtype=jnp.int32)
    pos_s = past + jnp.arange(ts, dtype=jnp.int32)
    yp, ys = x_prompt, x_sample
    kp_l, vp_l, hp_l, cp_l, fp_l = [], [], [], [], []
    ks_l, vs_l, hs_l, cs_l, fs_l = [], [], [], [], []
    for l in range(DEPTH):
        p = dict(w_in=w_in[l], diff_lambda=diff_lambda[l], subln_w=subln_w[l], lru_conv_w=lru_conv_w[l],
                 lru_conv_b=lru_conv_b[l], lru_gate_w=lru_gate_w[l], lru_gate_b=lru_gate_b[l],
                 lru_lambda=lru_lambda[l], w_branch=w_branch[l], w_out=w_out[l], ln1_g=ln1_g[l], ln1_b=ln1_b[l],
                 ffn_w_in=ffn_w_in[l], ffn_conv_w=ffn_conv_w[l], ffn_conv_b=ffn_conv_b[l],
                 ffn_w_down=ffn_w_down[l], ln2_g=ln2_g[l], ln2_b=ln2_b[l])
        lam_init = 0.8 - 0.6 * math.exp(-0.3 * l)
        h0 = jnp.zeros((bp, LRU_W), yp.dtype)
        cb0 = jnp.zeros((bp, LRU_CONV - 1, LRU_W), yp.dtype)
        fb0 = jnp.zeros((bp, FFN_CONV - 1, D_FF), yp.dtype)
        yp, kp, vp, hp, cp, fp = trunk_layer(yp, pos_p, prompt_attention, h0, cb0, fb0, p, lam_init)
        attend_s = functools.partial(sample_attention, cache_k=cache_k, cache_v=cache_v,
                                     page_table=page_table, layer=l)
        ys, k_s, v_s, h_s, c_s, f_s = trunk_layer(ys, pos_s, attend_s, state_lru_h[l].astype(ys.dtype),
                                                  state_lru_conv[l].astype(ys.dtype),
                                                  state_ffn_conv[l].astype(ys.dtype), p, lam_init)
        kp_l.append(kp); vp_l.append(vp); hp_l.append(hp); cp_l.append(cp); fp_l.append(fp)
        ks_l.append(k_s); vs_l.append(v_s); hs_l.append(h_s); cs_l.append(c_s); fs_l.append(f_s)
    return (yp, ys,
            jnp.stack(kp_l), jnp.stack(vp_l), jnp.stack(hp_l), jnp.stack(cp_l), jnp.stack(fp_l),
            jnp.stack(ks_l), jnp.stack(vs_l), jnp.stack(hs_l), jnp.stack(cs_l), jnp.stack(fs_l))
```

```python
import functools
import math

import jax
import jax.numpy as jnp
from jax import lax
from jax.experimental import pallas as pl
from jax.experimental.pallas import tpu as pltpu

ROPE_THETA = 500000.0
LRU_C = 8.0
NORM_EPS = 1e-5
LANE = 128
SUBLANE = 8
VMEM_LIMIT_BYTES = 56 * 1024 * 1024
MASK_VALUE = -1e30
PROJ_TM, PROJ_TN = 1024, 512
ATT_TQ, ATT_TK = 512, 1024
PAGES_PER_STEP = 4
LRU_TT, LRU_CB = 512, 512
MERGE_TM, MERGE_TN = 1024, 512
LN_TM, LN_TK = 512, 512
LN_NCHUNK, LN_RCHUNK = 512, 128
FFN_TM, FFN_TN = 1024, 512


def _cparams(sem):
    return pltpu.CompilerParams(dimension_semantics=sem, vmem_limit_bytes=VMEM_LIMIT_BYTES)


def _tile(n, pref, align):
    if n <= pref:
        return n
    t = (pref // align) * align
    while t >= align:
        if n % t == 0:
            return t
        t -= align
    raise ValueError(f"no tile for {n} (pref {pref}, align {align})")


def _bf16(x):
    return x.astype(jnp.bfloat16)


def _dot(a, b):
    return jnp.dot(a, b, preferred_element_type=jnp.float32)


def _dot_nt(a, b):
    return lax.dot_general(a, b, (((1,), (1,)), ((), ())), preferred_element_type=jnp.float32)


def _rope_table_kernel(a_ref, b_ref, c_ref, *, rot_dim, pos_base, pos_per_step):
    rows = a_ref.shape[0]
    half = rot_dim // 2
    lane = lax.broadcasted_iota(jnp.int32, (rows, LANE), 1)
    if pos_per_step:
        row = lax.broadcasted_iota(jnp.int32, (rows, LANE), 0)
        pos = pl.program_id(0) * rows + row
    else:
        pos = jnp.full((rows, LANE), pos_base, jnp.int32) + pl.program_id(0)
    fi = jnp.where(lane < half, lane, lane - half).astype(jnp.float32)
    inv_freq = jnp.exp(fi * (-(2.0 / rot_dim) * math.log(ROPE_THETA)))
    ang = pos.astype(jnp.float32) * inv_freq
    cos, sin = jnp.cos(ang), jnp.sin(ang)
    a_ref[...] = jnp.where(lane < rot_dim, cos, 1.0)
    b_ref[...] = jnp.where(lane < half, -sin, 0.0)
    c_ref[...] = jnp.where((lane >= half) & (lane < rot_dim), sin, 0.0)


def _rope_tables(n_steps, rows, rot_dim, pos_base, pos_per_step):
    shp = jax.ShapeDtypeStruct((n_steps * rows, LANE), jnp.float32)
    spec = pl.BlockSpec((rows, LANE), lambda s: (s, 0))
    return pl.pallas_call(
        functools.partial(_rope_table_kernel, rot_dim=rot_dim, pos_base=pos_base, pos_per_step=pos_per_step),
        grid=(n_steps,), out_shape=(shp, shp, shp), out_specs=(spec, spec, spec),
        compiler_params=_cparams(("arbitrary",)), name="rope_tables")()


def _gelu_tanh(x):
    return 0.5 * x * (1.0 + jnp.tanh(math.sqrt(2.0 / math.pi) * (x + 0.044715 * (x * x * x))))


def _proj_kernel(x_ref, w_ref, *rest, mode, want_f32, want_bf16, rot_half):
    acc = _dot(x_ref[...], w_ref[...])
    outs = rest[3:] if mode == "rope" else rest

    def emit(res, sl):
        i = 0
        if want_f32:
            outs[i][:, sl] = res
            i += 1
        if want_bf16:
            outs[i][:, sl] = _bf16(res)

    if mode == "rope":
        ta, tb, tc = rest[0][...], rest[1][...], rest[2][...]
        for c in range(acc.shape[1] // LANE):
            sl = slice(c * LANE, (c + 1) * LANE)
            xa = acc[:, sl]
            emit(xa * ta + pltpu.roll(xa, LANE - rot_half, 1) * tb + pltpu.roll(xa, rot_half, 1) * tc, sl)
    elif mode == "gelu":
        emit(_gelu_tanh(acc), slice(None))
    elif mode == "sigmoid":
        emit(jax.nn.sigmoid(acc), slice(None))
    else:
        emit(acc, slice(None))


def _proj(x16, w16, col_off, width, mode, want_f32, want_bf16, tables=None, rot_half=0):
    m, kdim = x16.shape
    tm = _tile(m, PROJ_TM, 16)
    tn = _tile(math.gcd(width, col_off) if col_off else width, PROJ_TN, LANE)
    off = col_off // tn
    in_specs = [pl.BlockSpec((tm, kdim), lambda i, j: (i, 0)),
                pl.BlockSpec((kdim, tn), lambda i, j: (0, j + off))]
    args = [x16, w16]
    if mode == "rope":
        period = tables[0].shape[0] // tm
        tspec = pl.BlockSpec((tm, LANE), lambda i, j: (i % period, 0))
        in_specs += [tspec, tspec, tspec]
        args += list(tables)
    out_shape, out_specs = [], []
    ospec = pl.BlockSpec((tm, tn), lambda i, j: (i, j))
    if want_f32:
        out_shape.append(jax.ShapeDtypeStruct((m, width), jnp.float32))
        out_specs.append(ospec)
    if want_bf16:
        out_shape.append(jax.ShapeDtypeStruct((m, width), jnp.bfloat16))
        out_specs.append(ospec)
    return pl.pallas_call(
        functools.partial(_proj_kernel, mode=mode, want_f32=want_f32, want_bf16=want_bf16, rot_half=rot_half),
        grid=(m // tm, width // tn), in_specs=in_specs, out_shape=tuple(out_shape), out_specs=tuple(out_specs),
        compiler_params=_cparams(("parallel", "arbitrary")), name=f"proj_{mode}")(*args)


def _diff_lambda(dl_ref, lam_init):
    lv = dl_ref[...]
    s01 = jnp.sum(lv[0:1] * lv[1:2], axis=-1, keepdims=True)
    s23 = jnp.sum(lv[2:3] * lv[3:4], axis=-1, keepdims=True)
    return jnp.exp(s01) - jnp.exp(s23) + lam_init


def _diff_finish(acc, l, lam, sub_w, lam_init, rows):
    o = acc[:rows] / l[:rows] - lam * (acc[rows:] / l[rows:])
    ms = jnp.mean(o * o, axis=-1, keepdims=True)
    return o * lax.rsqrt(ms + NORM_EPS) * sub_w * (1.0 - lam_init)


def _attn_prompt_kernel(dl_ref, sub_ref, q_ref, k_ref, v_ref, o_ref, m_sc, l_sc, acc_sc, *,
                        tq, tk, nk, group, hd, vd, scale, lam_init):
    qi = pl.program_id(2)
    ki = pl.program_id(3)
    rows = group * tq

    @pl.when(ki == 0)
    def _():
        m_sc[...] = jnp.full(m_sc.shape, MASK_VALUE, jnp.float32)
        l_sc[...] = jnp.zeros(l_sc.shape, jnp.float32)
        acc_sc[...] = jnp.zeros(acc_sc.shape, jnp.float32)

    def step(masked):
        v = v_ref[...]
        for c in range(2):
            qc = jnp.concatenate([q_ref[:, (g * 2 + c) * hd:(g * 2 + c + 1) * hd] for g in range(group)], axis=0)
            s = _dot_nt(qc, k_ref[:, c * hd:(c + 1) * hd]) * scale
            if masked:
                r = lax.broadcasted_iota(jnp.int32, (tq, tk), 0) + qi * tq
                col = lax.broadcasted_iota(jnp.int32, (tq, tk), 1) + ki * tk
                keep = jnp.concatenate([col <= r] * group, axis=0)
                s = jnp.where(keep, s, MASK_VALUE)
            m_prev = m_sc[c]
            m_new = jnp.maximum(m_prev, jnp.max(s, axis=-1, keepdims=True))
            alpha = jnp.exp(m_prev - m_new)
            p = jnp.exp(s - m_new)
            l_sc[c] = alpha * l_sc[c] + jnp.sum(p, axis=-1, keepdims=True)
            acc_sc[c] = alpha * acc_sc[c] + _dot(_bf16(p), v)
            m_sc[c] = m_new

    first_row = qi * tq
    last_row = first_row + tq - 1
    needed = ki * tk <= last_row
    full = ki * tk + tk - 1 <= first_row

    @pl.when(needed & full)
    def _():
        step(False)

    @pl.when(needed & jnp.logical_not(full))
    def _():
        step(True)

    @pl.when(ki == nk - 1)
    def _():
        lam = _diff_lambda(dl_ref, lam_init)
        acc = jnp.concatenate([acc_sc[0], acc_sc[1]], axis=0)
        l = jnp.concatenate([l_sc[0], l_sc[1]], axis=0)
        o = _diff_finish(acc, l, lam, sub_ref[...], lam_init, rows)
        for g in range(group):
            o_ref[:, g * vd:(g + 1) * vd] = _bf16(o[g * tq:(g + 1) * tq])


def _attn_prompt(q16, k16, v16, dl, sub_w, batch, seq, kvh, group, hd, vd, lam_init):
    tq = _tile(seq, ATT_TQ, 16)
    tk = _tile(seq, ATT_TK, LANE)
    nq, nk = seq // tq, seq // tk
    qw = group * 2 * hd

    def kv_map(b, h, qi, ki):
        return (b * nk + jnp.minimum(ki, (qi * tq + tq - 1) // tk), h)

    return pl.pallas_call(
        functools.partial(_attn_prompt_kernel, tq=tq, tk=tk, nk=nk, group=group, hd=hd, vd=vd,
                          scale=hd ** -0.5, lam_init=lam_init),
        grid=(batch, kvh, nq, nk),
        in_specs=[pl.BlockSpec(dl.shape, lambda b, h, qi, ki: (0, 0)),
                  pl.BlockSpec((1, vd), lambda b, h, qi, ki: (0, 0)),
                  pl.BlockSpec((tq, qw), lambda b, h, qi, ki: (b * nq + qi, h)),
                  pl.BlockSpec((tk, 2 * hd), kv_map),
                  pl.BlockSpec((tk, vd), kv_map)],
        out_shape=jax.ShapeDtypeStruct((batch * seq, kvh * group * vd), jnp.bfloat16),
        out_specs=pl.BlockSpec((tq, group * vd), lambda b, h, qi, ki: (b * nq + qi, h)),
        scratch_shapes=[pltpu.VMEM((2, group * tq, 1), jnp.float32),
                        pltpu.VMEM((2, group * tq, 1), jnp.float32),
                        pltpu.VMEM((2, group * tq, vd), jnp.float32)],
        compiler_params=_cparams(("parallel", "parallel", "parallel", "arbitrary")),
        name="attn_prompt")(dl, sub_w, q16, k16, v16)


def _attn_sample_kernel(pt_ref, dl_ref, sub_ref, q_ref, kn_ref, vn_ref, *rest, pps, n_steps, page, kvh, rq,
                        dec_seq, hd, vd, scale, lam_init):
    k_refs, v_refs = rest[:pps], rest[pps:2 * pps]
    o_ref = rest[2 * pps]
    kb_sc, vb_sc, m_sc, l_sc, acc_sc = rest[2 * pps + 1:]
    step_id = pl.program_id(1)

    @pl.when(step_id == 0)
    def _():
        m_sc[...] = jnp.full(m_sc.shape, MASK_VALUE, jnp.float32)
        l_sc[...] = jnp.zeros(l_sc.shape, jnp.float32)
        acc_sc[...] = jnp.zeros(acc_sc.shape, jnp.float32)

    def attend(kb, vb, keep):
        for h in range(kvh):
            s = jnp.concatenate(
                [_dot_nt(q_ref[0, h, c * rq:(c + 1) * rq, :], kb(slice((h * 2 + c) * hd, (h * 2 + c + 1) * hd)))
                 for c in range(2)], axis=0) * scale
            if keep is not None:
                s = jnp.where(keep, s, MASK_VALUE)
            m_prev = m_sc[h]
            m_new = jnp.maximum(m_prev, jnp.max(s, axis=-1, keepdims=True))
            alpha = jnp.exp(m_prev - m_new)
            p = jnp.exp(s - m_new)
            l_sc[h] = alpha * l_sc[h] + jnp.sum(p, axis=-1, keepdims=True)
            acc_sc[h] = alpha * acc_sc[h] + _dot(_bf16(p), vb(slice(h * vd, (h + 1) * vd)))
            m_sc[h] = m_new

    for p in range(pps):
        kb_sc[p * page:(p + 1) * page, :] = _bf16(k_refs[p][0])
        vb_sc[p * page:(p + 1) * page, :] = _bf16(v_refs[p][0])
    attend(lambda sl: kb_sc[:, sl], lambda sl: vb_sc[:, sl], None)

    @pl.when(step_id == n_steps - 1)
    def _():
        nkeys = kn_ref.shape[1]
        row_t = lax.broadcasted_iota(jnp.int32, (2 * rq, nkeys), 0) % dec_seq
        key_t = lax.broadcasted_iota(jnp.int32, (2 * rq, nkeys), 1)
        attend(lambda sl: kn_ref[0, :, sl], lambda sl: vn_ref[0, :, sl], key_t <= row_t)
        lam = _diff_lambda(dl_ref, lam_init)
        for h in range(kvh):
            o_ref[0, h] = _diff_finish(acc_sc[h], l_sc[h], lam, sub_ref[...], lam_init, rq)


def _attn_sample(q16, kn16, vn16, cache_k, cache_v, page_table, dl, sub_w, group, dec_seq, lam_init):
    n_pool, page, kvh, _, hd = cache_k.shape
    vd = cache_v.shape[-1]
    dec_b, n_pages = page_table.shape
    rq = group * dec_seq
    pps = math.gcd(n_pages, PAGES_PER_STEP)
    n_steps = n_pages // pps
    ck = cache_k.reshape(n_pool, page, kvh * 2 * hd)
    cv = cache_v.reshape(n_pool, page, kvh * vd)

    def page_map(p):
        return lambda b, s, pt: (pt[b * n_pages + s * pps + p], 0, 0)

    per_b = lambda b, s, pt: (b, 0, 0)
    in_specs = [pl.BlockSpec(dl.shape, lambda b, s, pt: (0, 0)),
                pl.BlockSpec((1, vd), lambda b, s, pt: (0, 0)),
                pl.BlockSpec((1,) + q16.shape[1:], lambda b, s, pt: (b, 0, 0, 0)),
                pl.BlockSpec((1,) + kn16.shape[1:], per_b),
                pl.BlockSpec((1,) + vn16.shape[1:], per_b)]
    in_specs += [pl.BlockSpec((1, page, kvh * 2 * hd), page_map(p)) for p in range(pps)]
    in_specs += [pl.BlockSpec((1, page, kvh * vd), page_map(p)) for p in range(pps)]
    grid_spec = pltpu.PrefetchScalarGridSpec(
        num_scalar_prefetch=1, grid=(dec_b, n_steps), in_specs=in_specs,
        out_specs=pl.BlockSpec((1, kvh, rq, vd), lambda b, s, pt: (b, 0, 0, 0)),
        scratch_shapes=[pltpu.VMEM((pps * page, kvh * 2 * hd), jnp.bfloat16),
                        pltpu.VMEM((pps * page, kvh * vd), jnp.bfloat16),
                        pltpu.VMEM((kvh, 2 * rq, 1), jnp.float32),
                        pltpu.VMEM((kvh, 2 * rq, 1), jnp.float32),
                        pltpu.VMEM((kvh, 2 * rq, vd), jnp.float32)])
    return pl.pallas_call(
        functools.partial(_attn_sample_kernel, pps=pps, n_steps=n_steps, page=page, kvh=kvh, rq=rq,
                          dec_seq=dec_seq, hd=hd, vd=vd, scale=hd ** -0.5, lam_init=lam_init),
        grid_spec=grid_spec,
        out_shape=jax.ShapeDtypeStruct((dec_b, kvh, rq, vd), jnp.float32),
        compiler_params=_cparams(("parallel", "arbitrary")),
        name="attn_sample")(page_table.reshape(-1), dl, sub_w, q16, kn16, vn16, *([ck] * pps), *([cv] * pps))


def _lru_gates(xc, gw_ref, gb_ref, lam_ref):
    blk = gw_ref.shape[-1]
    a_parts, u_parts = [], []
    sp = jax.nn.softplus(-lam_ref[...])
    for hd in range(xc.shape[1] // blk):
        sl = slice(hd * blk, (hd + 1) * blk)
        xh = xc[:, sl]
        xh16 = _bf16(xh)
        i_gate = jax.nn.sigmoid(_dot(xh16, gw_ref[0, hd]) + gb_ref[0:1, sl])
        r_gate = jax.nn.sigmoid(_dot(xh16, gw_ref[1, hd]) + gb_ref[1:2, sl])
        a = jnp.exp(-LRU_C * r_gate * sp[:, sl])
        a_parts.append(a)
        u_parts.append(jnp.sqrt(1.0 - a * a) * (i_gate * xh))
    cat = lambda ps: jnp.concatenate(ps, axis=1) if len(ps) > 1 else ps[0]
    return cat(a_parts), cat(u_parts)


def _lru_prompt_kernel(x_ref, yg_ref, cw_ref, cb_ref, gw_ref, gb_ref, lam_ref, h0_ref, buf_ref,
                       o_ref, ht_ref, ext_sc, a_sc, u_sc, hs_sc, h_sc, *, tt, kconv):
    ti = pl.program_id(2)
    pad = SUBLANE
    nprev = kconv - 1

    @pl.when(ti == 0)
    def _():
        ext_sc[pad - nprev:pad, :] = buf_ref[0]
        h_sc[...] = h0_ref[0]

    ext_sc[pad:pad + tt, :] = x_ref[...]
    xc = cb_ref[...] + cw_ref[nprev:kconv, :] * x_ref[...]
    for s in range(1, kconv):
        xc = xc + cw_ref[nprev - s:kconv - s, :] * ext_sc[pad - s:pad - s + tt, :]
    tail = ext_sc[pad + tt - nprev:pad + tt, :]
    ext_sc[pad - nprev:pad, :] = tail

    a, u = _lru_gates(xc, gw_ref, gb_ref, lam_ref)
    row = lax.broadcasted_iota(jnp.int32, a.shape, 0) % SUBLANE
    s = 1
    while s < SUBLANE:
        keep = row >= s
        a_sh = pltpu.roll(a, s, 0)
        u_sh = pltpu.roll(u, s, 0)
        u = jnp.where(keep, a * u_sh + u, u)
        a = jnp.where(keep, a * a_sh, a)
        s *= 2
    a_sc[...] = a
    u_sc[...] = u

    def body(g, h):
        r0 = pl.multiple_of(g * SUBLANE, SUBLANE)
        hrows = a_sc[pl.ds(r0, SUBLANE), :] * h + u_sc[pl.ds(r0, SUBLANE), :]
        hs_sc[pl.ds(r0, SUBLANE), :] = hrows
        return hrows[SUBLANE - 1:SUBLANE, :]

    h = lax.fori_loop(0, tt // SUBLANE, body, h_sc[...])
    h_sc[...] = h
    ht_ref[0] = h
    o_ref[...] = _bf16(hs_sc[...] * yg_ref[...])


def _lru_prompt(xb, yg, conv_w, conv_b, gw16, gate_b, lam, h0, buf0, batch, seq):
    c = xb.shape[1]
    kconv = conv_w.shape[0]
    blk = gw16.shape[-1]
    cb = _tile(c, LRU_CB, blk)
    tt = _tile(seq, LRU_TT, SUBLANE)
    nt = seq // tt
    chan = lambda b, ci, ti: (0, ci)
    return pl.pallas_call(
        functools.partial(_lru_prompt_kernel, tt=tt, kconv=kconv),
        grid=(batch, c // cb, nt),
        in_specs=[pl.BlockSpec((tt, cb), lambda b, ci, ti: (b * nt + ti, ci)),
                  pl.BlockSpec((tt, cb), lambda b, ci, ti: (b * nt + ti, ci)),
                  pl.BlockSpec((kconv, cb), chan),
                  pl.BlockSpec((1, cb), chan),
                  pl.BlockSpec((2, cb // blk, blk, blk), lambda b, ci, ti: (0, ci, 0, 0)),
                  pl.BlockSpec((2, cb), chan),
                  pl.BlockSpec((1, cb), chan),
                  pl.BlockSpec((1, 1, cb), lambda b, ci, ti: (b, 0, ci)),
                  pl.BlockSpec((1, kconv - 1, cb), lambda b, ci, ti: (b, 0, ci))],
        out_shape=(jax.ShapeDtypeStruct((batch * seq, c), jnp.bfloat16),
                   jax.ShapeDtypeStruct((batch, 1, c), jnp.float32)),
        out_specs=(pl.BlockSpec((tt, cb), lambda b, ci, ti: (b * nt + ti, ci)),
                   pl.BlockSpec((1, 1, cb), lambda b, ci, ti: (b, 0, ci))),
        scratch_shapes=[pltpu.VMEM((tt + SUBLANE, cb), jnp.float32),
                        pltpu.VMEM((tt, cb), jnp.float32),
                        pltpu.VMEM((tt, cb), jnp.float32),
                        pltpu.VMEM((tt, cb), jnp.float32),
                        pltpu.VMEM((1, cb), jnp.float32)],
        compiler_params=_cparams(("parallel", "parallel", "arbitrary")),
        name="lru_prompt")(xb, yg, conv_w, conv_b, gw16, gate_b, lam, h0, buf0)


def _lru_sample_kernel(x_ref, yg_ref, cw_ref, cb_ref, gw_ref, gb_ref, lam_ref, h0_ref, buf_ref,
                       o_ref, ht_ref, *, nb, nsteps, kconv):
    full = jnp.concatenate([buf_ref[...], x_ref[...]], axis=0)
    xc = cb_ref[...] + cw_ref[0:1, :] * full[0:nsteps * nb]
    for j in range(1, kconv):
        xc = xc + cw_ref[j:j + 1, :] * full[j * nb:(j + nsteps) * nb]
    a, u = _lru_gates(xc, gw_ref, gb_ref, lam_ref)
    h = h0_ref[...]
    for t in range(nsteps):
        sl = slice(t * nb, (t + 1) * nb)
        h = a[sl] * h + u[sl]
        o_ref[sl, :] = _bf16(h * yg_ref[sl, :])
    ht_ref[...] = h


def _lru_sample(xb, yg, conv_w, conv_b, gw16, gate_b, lam, h0, buf_tm, nb, nsteps):
    c = xb.shape[1]
    kconv = conv_w.shape[0]
    blk = gw16.shape[-1]
    cb = _tile(c, LRU_CB, blk)
    chan = lambda ci: (0, ci)
    return pl.pallas_call(
        functools.partial(_lru_sample_kernel, nb=nb, nsteps=nsteps, kconv=kconv),
        grid=(c // cb,),
        in_specs=[pl.BlockSpec((nsteps * nb, cb), chan),
                  pl.BlockSpec((nsteps * nb, cb), chan),
                  pl.BlockSpec((kconv, cb), chan),
                  pl.BlockSpec((1, cb), chan),
                  pl.BlockSpec((2, cb // blk, blk, blk), lambda ci: (0, ci, 0, 0)),
                  pl.BlockSpec((2, cb), chan),
                  pl.BlockSpec((1, cb), chan),
                  pl.BlockSpec((nb, cb), chan),
                  pl.BlockSpec(((kconv - 1) * nb, cb), chan)],
        out_shape=(jax.ShapeDtypeStruct((nsteps * nb, c), jnp.bfloat16),
                   jax.ShapeDtypeStruct((nb, c), jnp.float32)),
        out_specs=(pl.BlockSpec((nsteps * nb, cb), chan), pl.BlockSpec((nb, cb), chan)),
        compiler_params=_cparams(("parallel",)),
        name="lru_sample")(xb, yg, conv_w, conv_b, gw16, gate_b, lam, h0, buf_tm)


def _merge_kernel(a_ref, b_ref, w1_ref, w2_ref, g1_ref, g2_ref, o_ref):
    r1 = _dot(a_ref[...], w1_ref[...])
    r2 = _dot(b_ref[...], w2_ref[...])
    o_ref[...] = _bf16(g1_ref[...].astype(jnp.float32) * r1 + g2_ref[...].astype(jnp.float32) * r2)


def _merge(o_att16, o_lru16, wb16, g16):
    m, ka = o_att16.shape
    kb = o_lru16.shape[1]
    n = wb16.shape[1]
    tm = _tile(m, MERGE_TM, 16)
    tn = _tile(n, MERGE_TN, LANE)
    assert ka % kb == 0
    g2_off = n // tn
    return pl.pallas_call(
        _merge_kernel, grid=(m // tm, n // tn),
        in_specs=[pl.BlockSpec((tm, ka), lambda i, j: (i, 0)),
                  pl.BlockSpec((tm, kb), lambda i, j: (i, 0)),
                  pl.BlockSpec((ka, tn), lambda i, j: (0, j)),
                  pl.BlockSpec((kb, tn), lambda i, j: (ka // kb, j)),
                  pl.BlockSpec((tm, tn), lambda i, j: (i, j)),
                  pl.BlockSpec((tm, tn), lambda i, j: (i, j + g2_off))],
        out_shape=jax.ShapeDtypeStruct((m, n), jnp.bfloat16),
        out_specs=pl.BlockSpec((tm, tn), lambda i, j: (i, j)),
        compiler_params=_cparams(("parallel", "arbitrary")),
        name="merge")(o_att16, o_lru16, wb16, wb16, g16, g16)


def _mm_ln_kernel(a_ref, w_ref, res_ref, g_ref, b_ref, *rest, nk, alpha, want_bf16):
    o_ref = rest[0]
    k = pl.program_id(1)
    tm, n = o_ref.shape

    @pl.when(k == 0)
    def _():
        o_ref[...] = alpha * res_ref[...]

    for n0 in range(0, n, LN_NCHUNK):
        sl = slice(n0, min(n0 + LN_NCHUNK, n))
        o_ref[:, sl] += _dot(a_ref[...], w_ref[:, sl])

    @pl.when(k == nk - 1)
    def _():
        for r0 in range(0, tm, LN_RCHUNK):
            rs = slice(r0, min(r0 + LN_RCHUNK, tm))
            y = o_ref[rs, :]
            mu = jnp.mean(y, axis=-1, keepdims=True)
            yc = y - mu
            var = jnp.mean(yc * yc, axis=-1, keepdims=True)
            out = yc * lax.rsqrt(var + NORM_EPS) * g_ref[...] + b_ref[...]
            o_ref[rs, :] = out
            if want_bf16:
                rest[1][rs, :] = _bf16(out)


def _mm_ln(a16, w16, res, gain, bias, alpha, want_bf16):
    m, kdim = a16.shape
    n = w16.shape[1]
    tm = _tile(m, LN_TM, 16)
    tk = _tile(kdim, LN_TK, LANE)
    nk = kdim // tk
    row = pl.BlockSpec((tm, n), lambda i, k: (i, 0))
    vec = pl.BlockSpec((1, n), lambda i, k: (0, 0))
    out_shape = [jax.ShapeDtypeStruct((m, n), jnp.float32)]
    out_specs = [row]
    if want_bf16:
        out_shape.append(jax.ShapeDtypeStruct((m, n), jnp.bfloat16))
        out_specs.append(row)
    return pl.pallas_call(
        functools.partial(_mm_ln_kernel, nk=nk, alpha=alpha, want_bf16=want_bf16),
        grid=(m // tm, nk),
        in_specs=[pl.BlockSpec((tm, tk), lambda i, k: (i, k)),
                  pl.BlockSpec((tk, n), lambda i, k: (k, 0)),
                  pl.BlockSpec((tm, n), lambda i, k: (i, 0), pipeline_mode=pl.Buffered(1)),
                  vec, vec],
        out_shape=tuple(out_shape), out_specs=tuple(out_specs),
        compiler_params=_cparams(("parallel", "arbitrary")),
        name="mm_ln")(a16, w16, res, gain, bias)


def _ffn_up_prompt_kernel(x_ref, wu_ref, wp_ref, cw_ref, cb_ref, buf_ref, o_ref, tail_ref, ext_sc, *,
                          tm, tiles_per_seq, kconv):
    i = pl.program_id(1)
    pad = SUBLANE
    nprev = kconv - 1

    @pl.when(i % tiles_per_seq == 0)
    def _():
        ext_sc[pad - nprev:pad, :] = buf_ref[0]

    x = x_ref[...]
    u = _dot(x, wu_ref[...])
    up = _dot(x, wp_ref[...])
    ext_sc[pad:pad + tm, :] = u
    uc = cb_ref[...] + cw_ref[nprev:kconv, :] * u
    for s in range(1, kconv):
        uc = uc + cw_ref[nprev - s:kconv - s, :] * ext_sc[pad - s:pad - s + tm, :]
    tail = ext_sc[pad + tm - nprev:pad + tm, :]
    ext_sc[pad - nprev:pad, :] = tail
    tail_ref[0] = ext_sc[tm:tm + pad, :]
    o_ref[...] = _bf16(jax.nn.silu(uc) * up)


def _ffn_up_prompt(y16, wu16, wp16, conv_w, conv_b, buf0, batch, seq):
    m, kdim = y16.shape
    n = wu16.shape[1]
    kconv = conv_w.shape[0]
    tm = _tile(seq, FFN_TM, 16)
    tn = _tile(n, FFN_TN, LANE)
    tps = seq // tm
    col = lambda j, i: (0, j)
    return pl.pallas_call(
        functools.partial(_ffn_up_prompt_kernel, tm=tm, tiles_per_seq=tps, kconv=kconv),
        grid=(n // tn, m // tm),
        in_specs=[pl.BlockSpec((tm, kdim), lambda j, i: (i, 0)),
                  pl.BlockSpec((kdim, tn), col),
                  pl.BlockSpec((kdim, tn), col),
                  pl.BlockSpec((kconv, tn), col),
                  pl.BlockSpec((1, tn), col),
                  pl.BlockSpec((1, kconv - 1, tn), lambda j, i: (i // tps, 0, j))],
        out_shape=(jax.ShapeDtypeStruct((m, n), jnp.bfloat16),
                   jax.ShapeDtypeStruct((batch, SUBLANE, n), jnp.float32)),
        out_specs=(pl.BlockSpec((tm, tn), lambda j, i: (i, j)),
                   pl.BlockSpec((1, SUBLANE, tn), lambda j, i: (i // tps, 0, j))),
        scratch_shapes=[pltpu.VMEM((tm + SUBLANE, tn), jnp.float32)],
        compiler_params=_cparams(("parallel", "arbitrary")),
        name="ffn_up_prompt")(y16, wu16, wp16, conv_w, conv_b, buf0)


def _ffn_up_sample_kernel(x_ref, wu_ref, wp_ref, cw_ref, cb_ref, buf_ref, o_ref, tail_ref, *, nb, nsteps, kconv):
    x = x_ref[...]
    u = _dot(x, wu_ref[...])
    up = _dot(x, wp_ref[...])
    full = jnp.concatenate([buf_ref[...], u], axis=0)
    uc = cb_ref[...] + cw_ref[0:1, :] * full[0:nsteps * nb]
    for j in range(1, kconv):
        uc = uc + cw_ref[j:j + 1, :] * full[j * nb:(j + nsteps) * nb]
    tail_ref[...] = full[nsteps * nb:(nsteps + kconv - 1) * nb]
    o_ref[...] = _bf16(jax.nn.silu(uc) * up)


def _ffn_up_sample(y16, wu16, wp16, conv_w, conv_b, buf_tm, nb, nsteps):
    m, kdim = y16.shape
    n = wu16.shape[1]
    kconv = conv_w.shape[0]
    tn = _tile(n, FFN_TN, LANE)
    col = lambda j: (0, j)
    return pl.pallas_call(
        functools.partial(_ffn_up_sample_kernel, nb=nb, nsteps=nsteps, kconv=kconv),
        grid=(n // tn,),
        in_specs=[pl.BlockSpec((m, kdim), lambda j: (0, 0)),
                  pl.BlockSpec((kdim, tn), col),
                  pl.BlockSpec((kdim, tn), col),
                  pl.BlockSpec((kconv, tn), col),
                  pl.BlockSpec((1, tn), col),
                  pl.BlockSpec(((kconv - 1) * nb, tn), col)],
        out_shape=(jax.ShapeDtypeStruct((m, n), jnp.bfloat16),
                   jax.ShapeDtypeStruct(((kconv - 1) * nb, n), jnp.float32)),
        out_specs=(pl.BlockSpec((m, tn), col), pl.BlockSpec(((kconv - 1) * nb, tn), col)),
        compiler_params=_cparams(("parallel",)),
        name="ffn_up_sample")(y16, wu16, wp16, conv_w, conv_b, buf_tm)


def _pad_cols(a, n):
    return jnp.pad(a, [(0, 0)] * (a.ndim - 1) + [(0, n - a.shape[-1])])


def kernel(x_prompt, x_sample, cache_k, cache_v, page_table, state_lru_h, state_lru_conv, state_ffn_conv,
           w_in, diff_lambda, subln_w, lru_conv_w, lru_conv_b, lru_gate_w, lru_gate_b, lru_lambda,
           w_branch, w_out, ln1_g, ln1_b, ffn_w_in, ffn_conv_w, ffn_conv_b, ffn_w_down, ln2_g, ln2_b):
    depth, d_model, in_w = w_in.shape
    bp, tp, _ = x_prompt.shape
    bs, ts, _ = x_sample.shape
    n_pool, page, kvh, _, hd = cache_k.shape[1:]
    vd = cache_v.shape[-1]
    lru_w = lru_lambda.shape[-1]
    lru_heads, lru_blk = lru_gate_w.shape[2], lru_gate_w.shape[3]
    d_ff = ffn_conv_w.shape[-1]
    k_w, v_w = kvh * 2 * hd, kvh * vd
    q_w = in_w - k_w - v_w - 2 * lru_w - 2 * d_model
    n_heads = q_w // (2 * hd)
    group = n_heads // kvh
    att_out_w = n_heads * vd
    rot_dim = hd // 4
    past = page_table.shape[1] * page
    alpha = (2.0 * depth) ** 0.25
    o_k, o_v, o_xb, o_yb, o_g = q_w, q_w + k_w, q_w + k_w + v_w, q_w + k_w + v_w + lru_w, q_w + k_w + v_w + 2 * lru_w
    ff_align = max(FFN_TN, LN_TK)
    d_ff_pad = -(-d_ff // ff_align) * ff_align

    mp, ms = bp * tp, bs * ts
    tab_rows = _tile(tp, PROJ_TM, 16)
    tab_p = _rope_tables(tp // tab_rows, tab_rows, rot_dim, 0, True)
    tab_s = _rope_tables(ts, bs, rot_dim, past, False)

    yp = x_prompt.reshape(mp, d_model)
    ys = x_sample.transpose(1, 0, 2).reshape(ms, d_model)
    outs_p, outs_s = [], []
    for l in range(depth):
        lam_init = 0.8 - 0.6 * math.exp(-0.3 * l)
        w_in16 = _bf16(w_in[l])
        wb16 = _bf16(w_branch[l])
        wo16 = _bf16(w_out[l])
        wu16 = _pad_cols(_bf16(ffn_w_in[l, :, :d_ff]), d_ff_pad)
        wp16 = _pad_cols(_bf16(ffn_w_in[l, :, d_ff:]), d_ff_pad)
        wd16 = jnp.pad(_bf16(ffn_w_down[l]), ((0, d_ff_pad - d_ff), (0, 0)))
        fcw = _pad_cols(ffn_conv_w[l], d_ff_pad)
        fcb = _pad_cols(ffn_conv_b[l][None], d_ff_pad)
        gw16 = _bf16(lru_gate_w[l])
        gate_b = lru_gate_b[l].reshape(2, lru_w)
        dl = diff_lambda[l]
        sub_w = subln_w[l][None]
        common = dict(conv_w=lru_conv_w[l], conv_b=lru_conv_b[l][None], gw16=gw16, gate_b=gate_b,
                      lam=lru_lambda[l][None])

        def project(x16, tables):
            q16, = _proj(x16, w_in16, 0, q_w, "rope", False, True, tables, rot_dim // 2)
            k32, k16 = _proj(x16, w_in16, o_k, k_w, "rope", True, True, tables, rot_dim // 2)
            v32, v16 = _proj(x16, w_in16, o_v, v_w, "plain", True, True)
            xb, = _proj(x16, w_in16, o_xb, lru_w, "plain", True, False)
            yg, = _proj(x16, w_in16, o_yb, lru_w, "gelu", True, False)
            g16, = _proj(x16, w_in16, o_g, 2 * d_model, "sigmoid", False, True)
            return q16, k32, k16, v32, v16, xb, yg, g16

        def finish(x32, o_att16, o_lru16, g16, ffn_up):
            merged16 = _merge(o_att16, o_lru16, wb16, g16)
            y1, y1_16 = _mm_ln(merged16, wo16, x32, ln1_g[l][None], ln1_b[l][None], alpha, True)
            f16, tail = ffn_up(y1_16)
            y2, = _mm_ln(f16, wd16, y1, ln2_g[l][None], ln2_b[l][None], alpha, False)
            return y2, tail

        q16, k32, k16, v32, v16, xb, yg, g16 = project(_bf16(yp), tab_p)
        o_att16 = _attn_prompt(q16, k16, v16, dl, sub_w, bp, tp, kvh, group, hd, vd, lam_init)
        zeros = lambda *s: jnp.zeros(s, jnp.float32)
        o_lru16, ht = _lru_prompt(xb, yg, h0=zeros(bp, 1, lru_w), buf0=zeros(bp, lru_conv_w.shape[1] - 1, lru_w),
                                  batch=bp, seq=tp, **common)
        kf = ffn_conv_w.shape[1]
        yp, tail = finish(yp, o_att16, o_lru16, g16,
                          lambda a: _ffn_up_prompt(a, wu16, wp16, fcw, fcb, zeros(bp, kf - 1, d_ff_pad), bp, tp))
        kl = lru_conv_w.shape[1]
        outs_p.append((k32.reshape(bp, tp, kvh, 2, hd), v32.reshape(bp, tp, kvh, vd), ht.reshape(bp, lru_w),
                       xb.reshape(bp, tp, lru_w)[:, tp - (kl - 1):],
                       tail[:, SUBLANE - (kf - 1):, :d_ff]))

        q16, k32, k16, v32, v16, xb, yg, g16 = project(_bf16(ys), tab_s)
        rq = group * ts
        qs = q16.reshape(ts, bs, kvh, group, 2, hd).transpose(1, 2, 4, 3, 0, 5).reshape(bs, kvh, 2 * rq, hd)
        nk_pad = -(-ts // LANE) * LANE
        pad_t = lambda a: jnp.pad(a.reshape(ts, bs, -1).transpose(1, 0, 2), ((0, 0), (0, nk_pad - ts), (0, 0)))
        o_s = _attn_sample(qs, pad_t(k16), pad_t(v16), cache_k[l], cache_v[l], page_table, dl, sub_w,
                           group, ts, lam_init)
        o_att16 = _bf16(o_s.reshape(bs, kvh, group, ts, vd).transpose(3, 0, 1, 2, 4).reshape(ms, att_out_w))
        to_tm = lambda a: a.transpose(1, 0, 2).reshape(-1, a.shape[-1])
        o_lru16, ht = _lru_sample(xb, yg, h0=state_lru_h[l], buf_tm=to_tm(state_lru_conv[l]), nb=bs, nsteps=ts,
                                  **common)
        fbuf = _pad_cols(to_tm(state_ffn_conv[l]), d_ff_pad)
        ys, tail = finish(ys, o_att16, o_lru16, g16,
                          lambda a: _ffn_up_sample(a, wu16, wp16, fcw, fcb, fbuf, bs, ts))
        from_tm = lambda a, n: a.reshape(n, bs, -1).transpose(1, 0, 2)
        lru_full = jnp.concatenate([to_tm(state_lru_conv[l]), xb], axis=0)
        outs_s.append((from_tm(k32, ts).reshape(bs, ts, kvh, 2, hd), from_tm(v32, ts).reshape(bs, ts, kvh, vd), ht,
                       from_tm(lru_full[lru_full.shape[0] - (kl - 1) * bs:], kl - 1),
                       from_tm(tail, kf - 1)[:, :, :d_ff]))

    y_prompt = yp.reshape(bp, tp, d_model)
    y_sample = ys.reshape(ts, bs, d_model).transpose(1, 0, 2)
    stack = lambda outs, i: jnp.stack([o[i] for o in outs])
    return (y_prompt, y_sample,
            stack(outs_p, 0), stack(outs_p, 1), stack(outs_p, 2), stack(outs_p, 3), stack(outs_p, 4),
            stack(outs_s, 0), stack(outs_s, 1), stack(outs_s, 2), stack(outs_s, 3), stack(outs_s, 4))
```

```python
import functools
import math

import jax
import jax.numpy as jnp
from jax import lax
from jax.experimental import pallas as pl
from jax.experimental.pallas import tpu as pltpu

ROPE_THETA = 500000.0
LRU_C = 8.0
NORM_EPS = 1e-5
LANE = 128
SUBLANE = 8
VMEM_LIMIT_BYTES = 56 * 1024 * 1024
MASK_VALUE = -1e30
PROJ_TM, PROJ_TN = 1024, 512
ATT_TQ = 512
PAGES_PER_STEP = 8
LRU_TT, LRU_CB = 512, 512
MERGE_TM, MERGE_TN = 1024, 512
LN_TM, LN_TK = 512, 512
LN_NCHUNK, LN_RCHUNK = 512, 128
LN_VMEM_BUDGET = 47 * 1024 * 1024
FFN_TM, FFN_TN = 1024, 512


def _cparams(sem):
    return pltpu.CompilerParams(dimension_semantics=sem, vmem_limit_bytes=VMEM_LIMIT_BYTES)


def _tile(n, pref, align):
    if n <= pref:
        return n
    t = (pref // align) * align
    while t >= align:
        if n % t == 0:
            return t
        t -= align
    raise ValueError(f"no tile for {n} (pref {pref}, align {align})")


def _bf16(x):
    return x.astype(jnp.bfloat16)


def _dot(a, b):
    return jnp.dot(a, b, preferred_element_type=jnp.float32)


def _dot_nt(a, b):
    return lax.dot_general(a, b, (((1,), (1,)), ((), ())), preferred_element_type=jnp.float32)


def _rope_table_kernel(a_ref, b_ref, c_ref, *, rot_dim, pos_base, pos_per_step):
    rows = a_ref.shape[0]
    half = rot_dim // 2
    lane = lax.broadcasted_iota(jnp.int32, (rows, LANE), 1)
    if pos_per_step:
        row = lax.broadcasted_iota(jnp.int32, (rows, LANE), 0)
        pos = pl.program_id(0) * rows + row
    else:
        pos = jnp.full((rows, LANE), pos_base, jnp.int32) + pl.program_id(0)
    fi = jnp.where(lane < half, lane, lane - half).astype(jnp.float32)
    inv_freq = jnp.exp(fi * (-(2.0 / rot_dim) * math.log(ROPE_THETA)))
    ang = pos.astype(jnp.float32) * inv_freq
    cos, sin = jnp.cos(ang), jnp.sin(ang)
    a_ref[...] = jnp.where(lane < rot_dim, cos, 1.0)
    b_ref[...] = jnp.where(lane < half, -sin, 0.0)
    c_ref[...] = jnp.where((lane >= half) & (lane < rot_dim), sin, 0.0)


def _rope_tables(n_steps, rows, rot_dim, pos_base, pos_per_step):
    shp = jax.ShapeDtypeStruct((n_steps * rows, LANE), jnp.float32)
    spec = pl.BlockSpec((rows, LANE), lambda s: (s, 0))
    return pl.pallas_call(
        functools.partial(_rope_table_kernel, rot_dim=rot_dim, pos_base=pos_base, pos_per_step=pos_per_step),
        grid=(n_steps,), out_shape=(shp, shp, shp), out_specs=(spec, spec, spec),
        compiler_params=_cparams(("arbitrary",)), name="rope_tables")()


def _gelu_tanh(x):
    return 0.5 * x * (1.0 + jnp.tanh(math.sqrt(2.0 / math.pi) * (x + 0.044715 * (x * x * x))))


def _proj_kernel(x_ref, w_ref, *rest, mode, want_f32, want_bf16, rot_half):
    acc = _dot(x_ref[...], w_ref[...])
    outs = rest[3:] if mode == "rope" else rest

    def emit(res, sl):
        i = 0
        if want_f32:
            outs[i][:, sl] = res
            i += 1
        if want_bf16:
            outs[i][:, sl] = _bf16(res)

    if mode == "rope":
        ta, tb, tc = rest[0][...], rest[1][...], rest[2][...]
        for c in range(acc.shape[1] // LANE):
            sl = slice(c * LANE, (c + 1) * LANE)
            xa = acc[:, sl]
            emit(xa * ta + pltpu.roll(xa, LANE - rot_half, 1) * tb + pltpu.roll(xa, rot_half, 1) * tc, sl)
    elif mode == "gelu":
        emit(_gelu_tanh(acc), slice(None))
    elif mode == "sigmoid":
        emit(jax.nn.sigmoid(acc), slice(None))
    else:
        emit(acc, slice(None))


def _proj(x16, w16, col_off, width, mode, want_f32, want_bf16, tables=None, rot_half=0):
    m, kdim = x16.shape
    tm = _tile(m, PROJ_TM, 16)
    tn = _tile(math.gcd(width, col_off) if col_off else width, PROJ_TN, LANE)
    off = col_off // tn
    in_specs = [pl.BlockSpec((tm, kdim), lambda i, j: (i, 0)),
                pl.BlockSpec((kdim, tn), lambda i, j: (0, j + off))]
    args = [x16, w16]
    if mode == "rope":
        period = tables[0].shape[0] // tm
        tspec = pl.BlockSpec((tm, LANE), lambda i, j: (i % period, 0))
        in_specs += [tspec, tspec, tspec]
        args += list(tables)
    out_shape, out_specs = [], []
    ospec = pl.BlockSpec((tm, tn), lambda i, j: (i, j))
    if want_f32:
        out_shape.append(jax.ShapeDtypeStruct((m, width), jnp.float32))
        out_specs.append(ospec)
    if want_bf16:
        out_shape.append(jax.ShapeDtypeStruct((m, width), jnp.bfloat16))
        out_specs.append(ospec)
    return pl.pallas_call(
        functools.partial(_proj_kernel, mode=mode, want_f32=want_f32, want_bf16=want_bf16, rot_half=rot_half),
        grid=(m // tm, width // tn), in_specs=in_specs, out_shape=tuple(out_shape), out_specs=tuple(out_specs),
        compiler_params=_cparams(("parallel", "arbitrary")), name=f"proj_{mode}")(*args)


def _diff_lambda(dl_ref, lam_init):
    lv = dl_ref[...]
    s01 = jnp.sum(lv[0:1] * lv[1:2], axis=-1, keepdims=True)
    s23 = jnp.sum(lv[2:3] * lv[3:4], axis=-1, keepdims=True)
    return jnp.exp(s01) - jnp.exp(s23) + lam_init


def _lanes(x, width):
    reps = width // LANE
    return jnp.concatenate([x] * reps, axis=1) if reps > 1 else x


def _diff_finish(acc, l, lam, sub_w, lam_init, rows):
    o = acc[:rows] / l[:rows] - lam * (acc[rows:] / l[rows:])
    ms = jnp.mean(o * o, axis=-1, keepdims=True)
    return o * lax.rsqrt(ms + NORM_EPS) * sub_w * (1.0 - lam_init)


def _attn_prompt_kernel(dl_ref, sub_ref, q_ref, k_ref, v_ref, o_ref, m_sc, l_sc, acc_sc, *,
                        tq, group, hd, vd, scale, lam_init):
    qi = pl.program_id(2)
    rows = group * tq
    m_sc[...] = jnp.full(m_sc.shape, MASK_VALUE, jnp.float32)
    l_sc[...] = jnp.zeros(l_sc.shape, jnp.float32)
    acc_sc[...] = jnp.zeros(acc_sc.shape, jnp.float32)

    def chunk(ki, masked):
        k0 = pl.multiple_of(ki * tq, tq)
        v = v_ref[pl.ds(k0, tq), :]
        for c in range(2):
            qc = jnp.concatenate([q_ref[:, (g * 2 + c) * hd:(g * 2 + c + 1) * hd] for g in range(group)], axis=0)
            s = _dot_nt(qc, k_ref[pl.ds(k0, tq), c * hd:(c + 1) * hd]) * scale
            if masked:
                r = lax.broadcasted_iota(jnp.int32, (tq, tq), 0)
                col = lax.broadcasted_iota(jnp.int32, (tq, tq), 1)
                keep = jnp.concatenate([col <= r] * group, axis=0)
                s = jnp.where(keep, s, MASK_VALUE)
            m_prev = m_sc[c]
            m_new = jnp.maximum(m_prev, jnp.max(s, axis=-1, keepdims=True))
            alpha = jnp.exp(m_prev - m_new)
            p = jnp.exp(s - _lanes(m_new, tq))
            l_sc[c] = alpha * l_sc[c] + jnp.sum(p, axis=-1, keepdims=True)
            acc_sc[c] = _lanes(alpha, vd) * acc_sc[c] + _dot(_bf16(p), v)
            m_sc[c] = m_new

    def body(ki, carry):
        chunk(ki, False)
        return carry

    lax.fori_loop(0, qi, body, 0)
    chunk(qi, True)

    lam = _diff_lambda(dl_ref, lam_init)
    acc = jnp.concatenate([acc_sc[0], acc_sc[1]], axis=0)
    l = _lanes(jnp.concatenate([l_sc[0], l_sc[1]], axis=0), vd)
    o = _diff_finish(acc, l, lam, sub_ref[...], lam_init, rows)
    for g in range(group):
        o_ref[:, g * vd:(g + 1) * vd] = _bf16(o[g * tq:(g + 1) * tq])


def _attn_prompt(q16, k16, v16, dl, sub_w, batch, seq, kvh, group, hd, vd, lam_init):
    tq = _tile(seq, ATT_TQ, LANE)
    nq = seq // tq
    qw = group * 2 * hd
    return pl.pallas_call(
        functools.partial(_attn_prompt_kernel, tq=tq, group=group, hd=hd, vd=vd,
                          scale=hd ** -0.5, lam_init=lam_init),
        grid=(batch, kvh, nq),
        in_specs=[pl.BlockSpec(dl.shape, lambda b, h, qi: (0, 0)),
                  pl.BlockSpec((1, vd), lambda b, h, qi: (0, 0)),
                  pl.BlockSpec((tq, qw), lambda b, h, qi: (b * nq + qi, h)),
                  pl.BlockSpec((seq, 2 * hd), lambda b, h, qi: (b, h)),
                  pl.BlockSpec((seq, vd), lambda b, h, qi: (b, h))],
        out_shape=jax.ShapeDtypeStruct((batch * seq, kvh * group * vd), jnp.bfloat16),
        out_specs=pl.BlockSpec((tq, group * vd), lambda b, h, qi: (b * nq + qi, h)),
        scratch_shapes=[pltpu.VMEM((2, group * tq, LANE), jnp.float32),
                        pltpu.VMEM((2, group * tq, LANE), jnp.float32),
                        pltpu.VMEM((2, group * tq, vd), jnp.float32)],
        compiler_params=_cparams(("parallel", "parallel", "arbitrary")),
        name="attn_prompt")(dl, sub_w, q16, k16, v16)


def _attn_sample_kernel(pt_ref, dl_ref, sub_ref, q_ref, kn_ref, vn_ref, *rest, pps, n_steps, page, kvh, rq,
                        dec_seq, hd, vd, scale, lam_init):
    k_refs, v_refs = rest[:pps], rest[pps:2 * pps]
    o_ref = rest[2 * pps]
    kb_sc, vb_sc, m_sc, l_sc, acc_sc = rest[2 * pps + 1:]
    step_id = pl.program_id(1)

    @pl.when(step_id == 0)
    def _():
        m_sc[...] = jnp.full(m_sc.shape, MASK_VALUE, jnp.float32)
        l_sc[...] = jnp.zeros(l_sc.shape, jnp.float32)
        acc_sc[...] = jnp.zeros(acc_sc.shape, jnp.float32)

    hrows = 2 * rq

    def attend(kb, vb, keep):
        s = jnp.concatenate(
            [_dot_nt(q_ref[0, h, c * rq:(c + 1) * rq, :], kb(slice((h * 2 + c) * hd, (h * 2 + c + 1) * hd)))
             for h in range(kvh) for c in range(2)], axis=0) * scale
        if keep is not None:
            s = jnp.where(keep, s, MASK_VALUE)
        m_prev = m_sc[...]
        m_new = jnp.maximum(m_prev, jnp.max(s, axis=-1, keepdims=True))
        alpha = jnp.exp(m_prev - m_new)
        p = jnp.exp(s - m_new)
        l_sc[...] = alpha * l_sc[...] + jnp.sum(p, axis=-1, keepdims=True)
        m_sc[...] = m_new
        p16 = _bf16(p)
        pv = jnp.concatenate([_dot(p16[h * hrows:(h + 1) * hrows], vb(slice(h * vd, (h + 1) * vd)))
                              for h in range(kvh)], axis=0)
        acc_sc[...] = alpha * acc_sc[...] + pv

    for p in range(pps):
        kb_sc[p * page:(p + 1) * page, :] = _bf16(k_refs[p][0])
        vb_sc[p * page:(p + 1) * page, :] = _bf16(v_refs[p][0])
    attend(lambda sl: kb_sc[:, sl], lambda sl: vb_sc[:, sl], None)

    @pl.when(step_id == n_steps - 1)
    def _():
        nkeys = kn_ref.shape[1]
        row_t = lax.broadcasted_iota(jnp.int32, (kvh * hrows, nkeys), 0) % dec_seq
        key_t = lax.broadcasted_iota(jnp.int32, (kvh * hrows, nkeys), 1)
        attend(lambda sl: kn_ref[0, :, sl], lambda sl: vn_ref[0, :, sl], key_t <= row_t)
        lam = _diff_lambda(dl_ref, lam_init)
        for h in range(kvh):
            hs = slice(h * hrows, (h + 1) * hrows)
            o_ref[0, h] = _diff_finish(acc_sc[hs, :], l_sc[hs, :], lam, sub_ref[...], lam_init, rq)


def _attn_sample(q16, kn16, vn16, cache_k, cache_v, layer, page_table, dl, sub_w, group, dec_seq, lam_init):
    depth, n_pool, page, kvh, _, hd = cache_k.shape
    vd = cache_v.shape[-1]
    dec_b, n_pages = page_table.shape
    rq = group * dec_seq
    pps = math.gcd(n_pages, PAGES_PER_STEP)
    n_steps = n_pages // pps
    ck = cache_k.reshape(depth * n_pool, page, kvh * 2 * hd)
    cv = cache_v.reshape(depth * n_pool, page, kvh * vd)
    page0 = layer * n_pool

    def page_map(p):
        return lambda b, s, pt: (page0 + pt[b * n_pages + s * pps + p], 0, 0)

    per_b = lambda b, s, pt: (b, 0, 0)
    in_specs = [pl.BlockSpec(dl.shape, lambda b, s, pt: (0, 0)),
                pl.BlockSpec((1, vd), lambda b, s, pt: (0, 0)),
                pl.BlockSpec((1,) + q16.shape[1:], lambda b, s, pt: (b, 0, 0, 0)),
                pl.BlockSpec((1,) + kn16.shape[1:], per_b),
                pl.BlockSpec((1,) + vn16.shape[1:], per_b)]
    in_specs += [pl.BlockSpec((1, page, kvh * 2 * hd), page_map(p)) for p in range(pps)]
    in_specs += [pl.BlockSpec((1, page, kvh * vd), page_map(p)) for p in range(pps)]
    grid_spec = pltpu.PrefetchScalarGridSpec(
        num_scalar_prefetch=1, grid=(dec_b, n_steps), in_specs=in_specs,
        out_specs=pl.BlockSpec((1, kvh, rq, vd), lambda b, s, pt: (b, 0, 0, 0)),
        scratch_shapes=[pltpu.VMEM((pps * page, kvh * 2 * hd), jnp.bfloat16),
                        pltpu.VMEM((pps * page, kvh * vd), jnp.bfloat16),
                        pltpu.VMEM((kvh * 2 * rq, 1), jnp.float32),
                        pltpu.VMEM((kvh * 2 * rq, 1), jnp.float32),
                        pltpu.VMEM((kvh * 2 * rq, vd), jnp.float32)])
    return pl.pallas_call(
        functools.partial(_attn_sample_kernel, pps=pps, n_steps=n_steps, page=page, kvh=kvh, rq=rq,
                          dec_seq=dec_seq, hd=hd, vd=vd, scale=hd ** -0.5, lam_init=lam_init),
        grid_spec=grid_spec,
        out_shape=jax.ShapeDtypeStruct((dec_b, kvh, rq, vd), jnp.float32),
        compiler_params=_cparams(("parallel", "arbitrary")),
        name="attn_sample")(page_table.reshape(-1), dl, sub_w, q16, kn16, vn16, *([ck] * pps), *([cv] * pps))


def _lru_gates(xc, gw_ref, gb_ref, lam_ref):
    blk = gw_ref.shape[-1]
    a_parts, u_parts = [], []
    sp = jax.nn.softplus(-lam_ref[...])
    for hd in range(xc.shape[1] // blk):
        sl = slice(hd * blk, (hd + 1) * blk)
        xh = xc[:, sl]
        xh16 = _bf16(xh)
        i_gate = jax.nn.sigmoid(_dot(xh16, gw_ref[0, hd]) + gb_ref[0:1, sl])
        r_gate = jax.nn.sigmoid(_dot(xh16, gw_ref[1, hd]) + gb_ref[1:2, sl])
        a = jnp.exp(-LRU_C * r_gate * sp[:, sl])
        a_parts.append(a)
        u_parts.append(jnp.sqrt(1.0 - a * a) * (i_gate * xh))
    cat = lambda ps: jnp.concatenate(ps, axis=1) if len(ps) > 1 else ps[0]
    return cat(a_parts), cat(u_parts)


def _lru_prompt_kernel(x_ref, yg_ref, cw_ref, cb_ref, gw_ref, gb_ref, lam_ref, h0_ref, buf_ref,
                       o_ref, ht_ref, ext_sc, a_sc, u_sc, hs_sc, h_sc, *, tt, kconv):
    ti = pl.program_id(2)
    pad = SUBLANE
    nprev = kconv - 1

    @pl.when(ti == 0)
    def _():
        ext_sc[pad - nprev:pad, :] = buf_ref[0]
        h_sc[...] = h0_ref[0]

    ext_sc[pad:pad + tt, :] = x_ref[...]
    xc = cb_ref[...] + cw_ref[nprev:kconv, :] * x_ref[...]
    for s in range(1, kconv):
        xc = xc + cw_ref[nprev - s:kconv - s, :] * ext_sc[pad - s:pad - s + tt, :]
    tail = ext_sc[pad + tt - nprev:pad + tt, :]
    ext_sc[pad - nprev:pad, :] = tail

    a, u = _lru_gates(xc, gw_ref, gb_ref, lam_ref)
    row = lax.broadcasted_iota(jnp.int32, a.shape, 0) % SUBLANE
    s = 1
    while s < SUBLANE:
        keep = row >= s
        a_sh = pltpu.roll(a, s, 0)
        u_sh = pltpu.roll(u, s, 0)
        u = jnp.where(keep, a * u_sh + u, u)
        a = jnp.where(keep, a * a_sh, a)
        s *= 2
    a_sc[...] = a
    u_sc[...] = u

    def body(g, h):
        r0 = pl.multiple_of(g * SUBLANE, SUBLANE)
        hrows = a_sc[pl.ds(r0, SUBLANE), :] * h + u_sc[pl.ds(r0, SUBLANE), :]
        hs_sc[pl.ds(r0, SUBLANE), :] = hrows
        return hrows[SUBLANE - 1:SUBLANE, :]

    h = lax.fori_loop(0, tt // SUBLANE, body, h_sc[...])
    h_sc[...] = h
    ht_ref[0] = h
    o_ref[...] = _bf16(hs_sc[...] * yg_ref[...])


def _lru_prompt(xb, yg, conv_w, conv_b, gw16, gate_b, lam, h0, buf0, batch, seq):
    c = xb.shape[1]
    kconv = conv_w.shape[0]
    blk = gw16.shape[-1]
    cb = _tile(c, LRU_CB, blk)
    tt = _tile(seq, LRU_TT, SUBLANE)
    nt = seq // tt
    chan = lambda b, ci, ti: (0, ci)
    return pl.pallas_call(
        functools.partial(_lru_prompt_kernel, tt=tt, kconv=kconv),
        grid=(batch, c // cb, nt),
        in_specs=[pl.BlockSpec((tt, cb), lambda b, ci, ti: (b * nt + ti, ci)),
                  pl.BlockSpec((tt, cb), lambda b, ci, ti: (b * nt + ti, ci)),
                  pl.BlockSpec((kconv, cb), chan),
                  pl.BlockSpec((1, cb), chan),
                  pl.BlockSpec((2, cb // blk, blk, blk), lambda b, ci, ti: (0, ci, 0, 0)),
                  pl.BlockSpec((2, cb), chan),
                  pl.BlockSpec((1, cb), chan),
                  pl.BlockSpec((1, 1, cb), lambda b, ci, ti: (b, 0, ci)),
                  pl.BlockSpec((1, kconv - 1, cb), lambda b, ci, ti: (b, 0, ci))],
        out_shape=(jax.ShapeDtypeStruct((batch * seq, c), jnp.bfloat16),
                   jax.ShapeDtypeStruct((batch, 1, c), jnp.float32)),
        out_specs=(pl.BlockSpec((tt, cb), lambda b, ci, ti: (b * nt + ti, ci)),
                   pl.BlockSpec((1, 1, cb), lambda b, ci, ti: (b, 0, ci))),
        scratch_shapes=[pltpu.VMEM((tt + SUBLANE, cb), jnp.float32),
                        pltpu.VMEM((tt, cb), jnp.float32),
                        pltpu.VMEM((tt, cb), jnp.float32),
                        pltpu.VMEM((tt, cb), jnp.float32),
                        pltpu.VMEM((1, cb), jnp.float32)],
        compiler_params=_cparams(("parallel", "parallel", "arbitrary")),
        name="lru_prompt")(xb, yg, conv_w, conv_b, gw16, gate_b, lam, h0, buf0)


def _lru_sample_kernel(x_ref, yg_ref, cw_ref, cb_ref, gw_ref, gb_ref, lam_ref, h0_ref, buf_ref,
                       o_ref, ht_ref, *, nb, nsteps, kconv):
    full = jnp.concatenate([buf_ref[...], x_ref[...]], axis=0)
    xc = cb_ref[...] + cw_ref[0:1, :] * full[0:nsteps * nb]
    for j in range(1, kconv):
        xc = xc + cw_ref[j:j + 1, :] * full[j * nb:(j + nsteps) * nb]
    a, u = _lru_gates(xc, gw_ref, gb_ref, lam_ref)
    h = h0_ref[...]
    for t in range(nsteps):
        sl = slice(t * nb, (t + 1) * nb)
        h = a[sl] * h + u[sl]
        o_ref[sl, :] = _bf16(h * yg_ref[sl, :])
    ht_ref[...] = h


def _lru_sample(xb, yg, conv_w, conv_b, gw16, gate_b, lam, h0, buf_tm, nb, nsteps):
    c = xb.shape[1]
    kconv = conv_w.shape[0]
    blk = gw16.shape[-1]
    cb = _tile(c, LRU_CB, blk)
    chan = lambda ci: (0, ci)
    return pl.pallas_call(
        functools.partial(_lru_sample_kernel, nb=nb, nsteps=nsteps, kconv=kconv),
        grid=(c // cb,),
        in_specs=[pl.BlockSpec((nsteps * nb, cb), chan),
                  pl.BlockSpec((nsteps * nb, cb), chan),
                  pl.BlockSpec((kconv, cb), chan),
                  pl.BlockSpec((1, cb), chan),
                  pl.BlockSpec((2, cb // blk, blk, blk), lambda ci: (0, ci, 0, 0)),
                  pl.BlockSpec((2, cb), chan),
                  pl.BlockSpec((1, cb), chan),
                  pl.BlockSpec((nb, cb), chan),
                  pl.BlockSpec(((kconv - 1) * nb, cb), chan)],
        out_shape=(jax.ShapeDtypeStruct((nsteps * nb, c), jnp.bfloat16),
                   jax.ShapeDtypeStruct((nb, c), jnp.float32)),
        out_specs=(pl.BlockSpec((nsteps * nb, cb), chan), pl.BlockSpec((nb, cb), chan)),
        compiler_params=_cparams(("parallel",)),
        name="lru_sample")(xb, yg, conv_w, conv_b, gw16, gate_b, lam, h0, buf_tm)


def _merge_kernel(a_ref, b_ref, w1_ref, w2_ref, g1_ref, g2_ref, o_ref):
    r1 = _dot(a_ref[...], w1_ref[...])
    r2 = _dot(b_ref[...], w2_ref[...])
    o_ref[...] = _bf16(g1_ref[...].astype(jnp.float32) * r1 + g2_ref[...].astype(jnp.float32) * r2)


def _merge(o_att16, o_lru16, wb16, g16):
    m, ka = o_att16.shape
    kb = o_lru16.shape[1]
    n = wb16.shape[1]
    tm = _tile(m, MERGE_TM, 16)
    tn = _tile(n, MERGE_TN, LANE)
    assert ka % kb == 0
    g2_off = n // tn
    return pl.pallas_call(
        _merge_kernel, grid=(m // tm, n // tn),
        in_specs=[pl.BlockSpec((tm, ka), lambda i, j: (i, 0)),
                  pl.BlockSpec((tm, kb), lambda i, j: (i, 0)),
                  pl.BlockSpec((ka, tn), lambda i, j: (0, j)),
                  pl.BlockSpec((kb, tn), lambda i, j: (ka // kb, j)),
                  pl.BlockSpec((tm, tn), lambda i, j: (i, j)),
                  pl.BlockSpec((tm, tn), lambda i, j: (i, j + g2_off))],
        out_shape=jax.ShapeDtypeStruct((m, n), jnp.bfloat16),
        out_specs=pl.BlockSpec((tm, tn), lambda i, j: (i, j)),
        compiler_params=_cparams(("parallel", "arbitrary")),
        name="merge")(o_att16, o_lru16, wb16, wb16, g16, g16)


def _mm_ln_kernel(a_ref, w_ref, res_ref, g_ref, b_ref, *rest, nk, alpha, want_bf16):
    o_ref = rest[0]
    k = pl.program_id(1)
    tm, n = o_ref.shape

    @pl.when(k == 0)
    def _():
        o_ref[...] = alpha * res_ref[...]

    for n0 in range(0, n, LN_NCHUNK):
        sl = slice(n0, min(n0 + LN_NCHUNK, n))
        o_ref[:, sl] += _dot(a_ref[...], w_ref[:, sl])

    @pl.when(k == nk - 1)
    def _():
        for r0 in range(0, tm, LN_RCHUNK):
            rs = slice(r0, min(r0 + LN_RCHUNK, tm))
            y = o_ref[rs, :]
            mu = jnp.mean(y, axis=-1, keepdims=True)
            yc = y - mu
            var = jnp.mean(yc * yc, axis=-1, keepdims=True)
            out = yc * lax.rsqrt(var + NORM_EPS) * g_ref[...] + b_ref[...]
            o_ref[rs, :] = out
            if want_bf16:
                rest[1][rs, :] = _bf16(out)


def _mm_ln(a16, w16, res, gain, bias, alpha, want_bf16):
    m, kdim = a16.shape
    n = w16.shape[1]
    tm = _tile(m, LN_TM, 16)
    fixed = tm * n * (2 * 4 + (2 * 2 if want_bf16 else 0) + 4) + 2 * tm * LN_NCHUNK * 4
    tk = _tile(kdim, LN_TK, LANE)
    while tk * 2 <= kdim and kdim % (tk * 2) == 0 and fixed + 2 * (tk * 2) * (n + tm) * 2 <= LN_VMEM_BUDGET:
        tk *= 2
    nk = kdim // tk
    row = pl.BlockSpec((tm, n), lambda i, k: (i, 0))
    vec = pl.BlockSpec((1, n), lambda i, k: (0, 0))
    out_shape = [jax.ShapeDtypeStruct((m, n), jnp.float32)]
    out_specs = [row]
    if want_bf16:
        out_shape.append(jax.ShapeDtypeStruct((m, n), jnp.bfloat16))
        out_specs.append(row)
    return pl.pallas_call(
        functools.partial(_mm_ln_kernel, nk=nk, alpha=alpha, want_bf16=want_bf16),
        grid=(m // tm, nk),
        in_specs=[pl.BlockSpec((tm, tk), lambda i, k: (i, k)),
                  pl.BlockSpec((tk, n), lambda i, k: (k, 0)),
                  pl.BlockSpec((tm, n), lambda i, k: (i, 0), pipeline_mode=pl.Buffered(1)),
                  vec, vec],
        out_shape=tuple(out_shape), out_specs=tuple(out_specs),
        compiler_params=_cparams(("parallel", "arbitrary")),
        name="mm_ln")(a16, w16, res, gain, bias)


def _ffn_up_prompt_kernel(x_ref, wu_ref, wp_ref, cw_ref, cb_ref, buf_ref, o_ref, tail_ref, ext_sc, *,
                          tm, tiles_per_seq, kconv):
    i = pl.program_id(1)
    pad = SUBLANE
    nprev = kconv - 1

    @pl.when(i % tiles_per_seq == 0)
    def _():
        ext_sc[pad - nprev:pad, :] = buf_ref[0]

    x = x_ref[...]
    u = _dot(x, wu_ref[...])
    up = _dot(x, wp_ref[...])
    ext_sc[pad:pad + tm, :] = u
    uc = cb_ref[...] + cw_ref[nprev:kconv, :] * u
    for s in range(1, kconv):
        uc = uc + cw_ref[nprev - s:kconv - s, :] * ext_sc[pad - s:pad - s + tm, :]
    tail = ext_sc[pad + tm - nprev:pad + tm, :]
    ext_sc[pad - nprev:pad, :] = tail
    tail_ref[0] = ext_sc[tm:tm + pad, :]
    o_ref[...] = _bf16(jax.nn.silu(uc) * up)


def _ffn_up_prompt(y16, wu16, wp16, conv_w, conv_b, buf0, batch, seq):
    m, kdim = y16.shape
    n = wu16.shape[1]
    kconv = conv_w.shape[0]
    tm = _tile(seq, FFN_TM, 16)
    tn = _tile(n, FFN_TN, LANE)
    tps = seq // tm
    col = lambda j, i: (0, j)
    return pl.pallas_call(
        functools.partial(_ffn_up_prompt_kernel, tm=tm, tiles_per_seq=tps, kconv=kconv),
        grid=(n // tn, m // tm),
        in_specs=[pl.BlockSpec((tm, kdim), lambda j, i: (i, 0)),
                  pl.BlockSpec((kdim, tn), col),
                  pl.BlockSpec((kdim, tn), col),
                  pl.BlockSpec((kconv, tn), col),
                  pl.BlockSpec((1, tn), col),
                  pl.BlockSpec((1, kconv - 1, tn), lambda j, i: (i // tps, 0, j))],
        out_shape=(jax.ShapeDtypeStruct((m, n), jnp.bfloat16),
                   jax.ShapeDtypeStruct((batch, SUBLANE, n), jnp.float32)),
        out_specs=(pl.BlockSpec((tm, tn), lambda j, i: (i, j)),
                   pl.BlockSpec((1, SUBLANE, tn), lambda j, i: (i // tps, 0, j))),
        scratch_shapes=[pltpu.VMEM((tm + SUBLANE, tn), jnp.float32)],
        compiler_params=_cparams(("parallel", "arbitrary")),
        name="ffn_up_prompt")(y16, wu16, wp16, conv_w, conv_b, buf0)


def _ffn_up_sample_kernel(x_ref, wu_ref, wp_ref, cw_ref, cb_ref, buf_ref, o_ref, tail_ref, *, nb, nsteps, kconv):
    x = x_ref[...]
    u = _dot(x, wu_ref[...])
    up = _dot(x, wp_ref[...])
    full = jnp.concatenate([buf_ref[...], u], axis=0)
    uc = cb_ref[...] + cw_ref[0:1, :] * full[0:nsteps * nb]
    for j in range(1, kconv):
        uc = uc + cw_ref[j:j + 1, :] * full[j * nb:(j + nsteps) * nb]
    tail_ref[...] = full[nsteps * nb:(nsteps + kconv - 1) * nb]
    o_ref[...] = _bf16(jax.nn.silu(uc) * up)


def _ffn_up_sample(y16, wu16, wp16, conv_w, conv_b, buf_tm, nb, nsteps):
    m, kdim = y16.shape
    n = wu16.shape[1]
    kconv = conv_w.shape[0]
    tn = _tile(n, FFN_TN, LANE)
    col = lambda j: (0, j)
    return pl.pallas_call(
        functools.partial(_ffn_up_sample_kernel, nb=nb, nsteps=nsteps, kconv=kconv),
        grid=(n // tn,),
        in_specs=[pl.BlockSpec((m, kdim), lambda j: (0, 0)),
                  pl.BlockSpec((kdim, tn), col),
                  pl.BlockSpec((kdim, tn), col),
                  pl.BlockSpec((kconv, tn), col),
                  pl.BlockSpec((1, tn), col),
                  pl.BlockSpec(((kconv - 1) * nb, tn), col)],
        out_shape=(jax.ShapeDtypeStruct((m, n), jnp.bfloat16),
                   jax.ShapeDtypeStruct(((kconv - 1) * nb, n), jnp.float32)),
        out_specs=(pl.BlockSpec((m, tn), col), pl.BlockSpec(((kconv - 1) * nb, tn), col)),
        compiler_params=_cparams(("parallel",)),
        name="ffn_up_sample")(y16, wu16, wp16, conv_w, conv_b, buf_tm)


def _pad_cols(a, n):
    return jnp.pad(a, [(0, 0)] * (a.ndim - 1) + [(0, n - a.shape[-1])])


def _cast_pad_kernel(x_ref, o_ref, *, nrb, ncb):
    valid = (pl.program_id(0) < nrb) & (pl.program_id(1) < ncb)

    @pl.when(valid)
    def _():
        o_ref[...] = _bf16(x_ref[0])

    @pl.when(jnp.logical_not(valid))
    def _():
        o_ref[...] = jnp.zeros(o_ref.shape, o_ref.dtype)


def _cast_pad(w, layer, col_off, width, rows_out, cols_out):
    rows = w.shape[1]
    rb = _tile(math.gcd(rows, rows_out), 1024, 16)
    cb = _tile(math.gcd(math.gcd(width, cols_out), col_off) if col_off else math.gcd(width, cols_out), 1024, LANE)
    nrb, ncb, cob = rows // rb, width // cb, col_off // cb
    return pl.pallas_call(
        functools.partial(_cast_pad_kernel, nrb=nrb, ncb=ncb),
        grid=(rows_out // rb, cols_out // cb),
        in_specs=[pl.BlockSpec((1, rb, cb),
                               lambda i, j: (layer, jnp.minimum(i, nrb - 1), jnp.minimum(j, ncb - 1) + cob))],
        out_shape=jax.ShapeDtypeStruct((rows_out, cols_out), jnp.bfloat16),
        out_specs=pl.BlockSpec((rb, cb), lambda i, j: (i, j)),
        compiler_params=_cparams(("parallel", "parallel")),
        name="cast_pad")(w)


def kernel(x_prompt, x_sample, cache_k, cache_v, page_table, state_lru_h, state_lru_conv, state_ffn_conv,
           w_in, diff_lambda, subln_w, lru_conv_w, lru_conv_b, lru_gate_w, lru_gate_b, lru_lambda,
           w_branch, w_out, ln1_g, ln1_b, ffn_w_in, ffn_conv_w, ffn_conv_b, ffn_w_down, ln2_g, ln2_b):
    depth, d_model, in_w = w_in.shape
    bp, tp, _ = x_prompt.shape
    bs, ts, _ = x_sample.shape
    n_pool, page, kvh, _, hd = cache_k.shape[1:]
    vd = cache_v.shape[-1]
    lru_w = lru_lambda.shape[-1]
    lru_heads, lru_blk = lru_gate_w.shape[2], lru_gate_w.shape[3]
    d_ff = ffn_conv_w.shape[-1]
    k_w, v_w = kvh * 2 * hd, kvh * vd
    q_w = in_w - k_w - v_w - 2 * lru_w - 2 * d_model
    n_heads = q_w // (2 * hd)
    group = n_heads // kvh
    att_out_w = n_heads * vd
    rot_dim = hd // 4
    past = page_table.shape[1] * page
    alpha = (2.0 * depth) ** 0.25
    o_k, o_v, o_xb, o_yb, o_g = q_w, q_w + k_w, q_w + k_w + v_w, q_w + k_w + v_w + lru_w, q_w + k_w + v_w + 2 * lru_w
    ff_align = max(FFN_TN, LN_TK)
    d_ff_pad = -(-d_ff // ff_align) * ff_align

    mp, ms = bp * tp, bs * ts
    tab_rows = _tile(tp, PROJ_TM, 16)
    tab_p = _rope_tables(tp // tab_rows, tab_rows, rot_dim, 0, True)
    tab_s = _rope_tables(ts, bs, rot_dim, past, False)

    yp = x_prompt.reshape(mp, d_model)
    ys = x_sample.transpose(1, 0, 2).reshape(ms, d_model)
    outs_p, outs_s = [], []
    for l in range(depth):
        lam_init = 0.8 - 0.6 * math.exp(-0.3 * l)
        w_in16 = _bf16(w_in[l])
        wb16 = _bf16(w_branch[l])
        wo16 = _bf16(w_out[l])
        wu16 = _cast_pad(ffn_w_in, l, 0, d_ff, d_model, d_ff_pad)
        wp16 = _cast_pad(ffn_w_in, l, d_ff, d_ff, d_model, d_ff_pad)
        wd16 = _cast_pad(ffn_w_down, l, 0, d_model, d_ff_pad, d_model)
        fcw = _pad_cols(ffn_conv_w[l], d_ff_pad)
        fcb = _pad_cols(ffn_conv_b[l][None], d_ff_pad)
        gw16 = _bf16(lru_gate_w[l])
        gate_b = lru_gate_b[l].reshape(2, lru_w)
        dl = diff_lambda[l]
        sub_w = subln_w[l][None]
        common = dict(conv_w=lru_conv_w[l], conv_b=lru_conv_b[l][None], gw16=gw16, gate_b=gate_b,
                      lam=lru_lambda[l][None])

        def project(x16, tables):
            q16, = _proj(x16, w_in16, 0, q_w, "rope", False, True, tables, rot_dim // 2)
            k32, k16 = _proj(x16, w_in16, o_k, k_w, "rope", True, True, tables, rot_dim // 2)
            v32, v16 = _proj(x16, w_in16, o_v, v_w, "plain", True, True)
            xb, = _proj(x16, w_in16, o_xb, lru_w, "plain", True, False)
            yg, = _proj(x16, w_in16, o_yb, lru_w, "gelu", True, False)
            g16, = _proj(x16, w_in16, o_g, 2 * d_model, "sigmoid", False, True)
            return q16, k32, k16, v32, v16, xb, yg, g16

        def finish(x32, o_att16, o_lru16, g16, ffn_up):
            merged16 = _merge(o_att16, o_lru16, wb16, g16)
            y1, y1_16 = _mm_ln(merged16, wo16, x32, ln1_g[l][None], ln1_b[l][None], alpha, True)
            f16, tail = ffn_up(y1_16)
            y2, = _mm_ln(f16, wd16, y1, ln2_g[l][None], ln2_b[l][None], alpha, False)
            return y2, tail

        q16, k32, k16, v32, v16, xb, yg, g16 = project(_bf16(yp), tab_p)
        o_att16 = _attn_prompt(q16, k16, v16, dl, sub_w, bp, tp, kvh, group, hd, vd, lam_init)
        zeros = lambda *s: jnp.zeros(s, jnp.float32)
        o_lru16, ht = _lru_prompt(xb, yg, h0=zeros(bp, 1, lru_w), buf0=zeros(bp, lru_conv_w.shape[1] - 1, lru_w),
                                  batch=bp, seq=tp, **common)
        kf = ffn_conv_w.shape[1]
        yp, tail = finish(yp, o_att16, o_lru16, g16,
                          lambda a: _ffn_up_prompt(a, wu16, wp16, fcw, fcb, zeros(bp, kf - 1, d_ff_pad), bp, tp))
        kl = lru_conv_w.shape[1]
        outs_p.append((k32.reshape(bp, tp, kvh, 2, hd), v32.reshape(bp, tp, kvh, vd), ht.reshape(bp, lru_w),
                       xb.reshape(bp, tp, lru_w)[:, tp - (kl - 1):],
                       tail[:, SUBLANE - (kf - 1):, :d_ff]))

        q16, k32, k16, v32, v16, xb, yg, g16 = project(_bf16(ys), tab_s)
        rq = group * ts
        qs = q16.reshape(ts, bs, kvh, group, 2, hd).transpose(1, 2, 4, 3, 0, 5).reshape(bs, kvh, 2 * rq, hd)
        nk_pad = -(-ts // LANE) * LANE
        pad_t = lambda a: jnp.pad(a.reshape(ts, bs, -1).transpose(1, 0, 2), ((0, 0), (0, nk_pad - ts), (0, 0)))
        o_s = _attn_sample(qs, pad_t(k16), pad_t(v16), cache_k, cache_v, l, page_table, dl, sub_w,
                           group, ts, lam_init)
        o_att16 = _bf16(o_s.reshape(bs, kvh, group, ts, vd).transpose(3, 0, 1, 2, 4).reshape(ms, att_out_w))
        to_tm = lambda a: a.transpose(1, 0, 2).reshape(-1, a.shape[-1])
        o_lru16, ht = _lru_sample(xb, yg, h0=state_lru_h[l], buf_tm=to_tm(state_lru_conv[l]), nb=bs, nsteps=ts,
                                  **common)
        fbuf = _pad_cols(to_tm(state_ffn_conv[l]), d_ff_pad)
        ys, tail = finish(ys, o_att16, o_lru16, g16,
                          lambda a: _ffn_up_sample(a, wu16, wp16, fcw, fcb, fbuf, bs, ts))
        from_tm = lambda a, n: a.reshape(n, bs, -1).transpose(1, 0, 2)
        lru_full = jnp.concatenate([to_tm(state_lru_conv[l]), xb], axis=0)
        outs_s.append((from_tm(k32, ts).reshape(bs, ts, kvh, 2, hd), from_tm(v32, ts).reshape(bs, ts, kvh, vd), ht,
                       from_tm(lru_full[lru_full.shape[0] - (kl - 1) * bs:], kl - 1),
                       from_tm(tail, kf - 1)[:, :, :d_ff]))

    y_prompt = yp.reshape(bp, tp, d_model)
    y_sample = ys.reshape(ts, bs, d_model).transpose(1, 0, 2)
    stack = lambda outs, i: jnp.stack([o[i] for o in outs])
    return (y_prompt, y_sample,
            stack(outs_p, 0), stack(outs_p, 1), stack(outs_p, 2), stack(outs_p, 3), stack(outs_p, 4),
            stack(outs_s, 0), stack(outs_s, 1), stack(outs_s, 2), stack(outs_s, 3), stack(outs_s, 4))
```

```python
import functools
import math

import jax
import jax.numpy as jnp
from jax import lax
from jax.experimental import pallas as pl
from jax.experimental.pallas import tpu as pltpu

ROPE_THETA = 500000.0
LRU_C = 8.0
NORM_EPS = 1e-5
LANE = 128
SUBLANE = 8
VMEM_LIMIT_BYTES = 56 * 1024 * 1024
MASK_VALUE = -1e30
PROJ_TM, PROJ_TN = 1024, 512
ATT_TQ = 512
PAGES_PER_STEP = 8
LRU_TT, LRU_CB = 512, 512
MERGE_TM, MERGE_TN = 1024, 512
LN_TM, LN_TK = 512, 512
LN_NCHUNK, LN_RCHUNK = 512, 128
LN_VMEM_BUDGET = 47 * 1024 * 1024
FFN_TM, FFN_TN = 1024, 512


def _cparams(sem):
    return pltpu.CompilerParams(dimension_semantics=sem, vmem_limit_bytes=VMEM_LIMIT_BYTES)


def _tile(n, pref, align):
    if n <= pref:
        return n
    t = (pref // align) * align
    while t >= align:
        if n % t == 0:
            return t
        t -= align
    raise ValueError(f"no tile for {n} (pref {pref}, align {align})")


def _bf16(x):
    return x.astype(jnp.bfloat16)


def _dot(a, b):
    return jnp.dot(a, b, preferred_element_type=jnp.float32)


def _dot_nt(a, b):
    return lax.dot_general(a, b, (((1,), (1,)), ((), ())), preferred_element_type=jnp.float32)


def _rope_table_kernel(a_ref, b_ref, c_ref, *, rot_dim, pos_base, pos_per_step):
    rows = a_ref.shape[0]
    half = rot_dim // 2
    lane = lax.broadcasted_iota(jnp.int32, (rows, LANE), 1)
    if pos_per_step:
        row = lax.broadcasted_iota(jnp.int32, (rows, LANE), 0)
        pos = pl.program_id(0) * rows + row
    else:
        pos = jnp.full((rows, LANE), pos_base, jnp.int32) + pl.program_id(0)
    fi = jnp.where(lane < half, lane, lane - half).astype(jnp.float32)
    inv_freq = jnp.exp(fi * (-(2.0 / rot_dim) * math.log(ROPE_THETA)))
    ang = pos.astype(jnp.float32) * inv_freq
    cos, sin = jnp.cos(ang), jnp.sin(ang)
    a_ref[...] = jnp.where(lane < rot_dim, cos, 1.0)
    b_ref[...] = jnp.where(lane < half, -sin, 0.0)
    c_ref[...] = jnp.where((lane >= half) & (lane < rot_dim), sin, 0.0)


def _rope_tables(n_steps, rows, rot_dim, pos_base, pos_per_step):
    shp = jax.ShapeDtypeStruct((n_steps * rows, LANE), jnp.float32)
    spec = pl.BlockSpec((rows, LANE), lambda s: (s, 0))
    return pl.pallas_call(
        functools.partial(_rope_table_kernel, rot_dim=rot_dim, pos_base=pos_base, pos_per_step=pos_per_step),
        grid=(n_steps,), out_shape=(shp, shp, shp), out_specs=(spec, spec, spec),
        compiler_params=_cparams(("arbitrary",)), name="rope_tables")()


def _gelu_tanh(x):
    return 0.5 * x * (1.0 + jnp.tanh(math.sqrt(2.0 / math.pi) * (x + 0.044715 * (x * x * x))))


def _proj_kernel(x_ref, w_ref, *rest, mode, want_f32, want_bf16, rot_half):
    acc = _dot(x_ref[...], w_ref[...])
    outs = rest[3:] if mode == "rope" else rest

    def emit(res, sl):
        i = 0
        if want_f32:
            outs[i][:, sl] = res
            i += 1
        if want_bf16:
            outs[i][:, sl] = _bf16(res)

    if mode == "rope":
        ta, tb, tc = rest[0][...], rest[1][...], rest[2][...]
        for c in range(acc.shape[1] // LANE):
            sl = slice(c * LANE, (c + 1) * LANE)
            xa = acc[:, sl]
            emit(xa * ta + pltpu.roll(xa, LANE - rot_half, 1) * tb + pltpu.roll(xa, rot_half, 1) * tc, sl)
    elif mode == "gelu":
        emit(_gelu_tanh(acc), slice(None))
    elif mode == "sigmoid":
        emit(jax.nn.sigmoid(acc), slice(None))
    else:
        emit(acc, slice(None))


def _proj(x16, w16, col_off, width, mode, want_f32, want_bf16, tables=None, rot_half=0):
    m, kdim = x16.shape
    tm = _tile(m, PROJ_TM, 16)
    tn = _tile(math.gcd(width, col_off) if col_off else width, PROJ_TN, LANE)
    off = col_off // tn
    in_specs = [pl.BlockSpec((tm, kdim), lambda i, j: (i, 0)),
                pl.BlockSpec((kdim, tn), lambda i, j: (0, j + off))]
    args = [x16, w16]
    if mode == "rope":
        period = tables[0].shape[0] // tm
        tspec = pl.BlockSpec((tm, LANE), lambda i, j: (i % period, 0))
        in_specs += [tspec, tspec, tspec]
        args += list(tables)
    out_shape, out_specs = [], []
    ospec = pl.BlockSpec((tm, tn), lambda i, j: (i, j))
    if want_f32:
        out_shape.append(jax.ShapeDtypeStruct((m, width), jnp.float32))
        out_specs.append(ospec)
    if want_bf16:
        out_shape.append(jax.ShapeDtypeStruct((m, width), jnp.bfloat16))
        out_specs.append(ospec)
    return pl.pallas_call(
        functools.partial(_proj_kernel, mode=mode, want_f32=want_f32, want_bf16=want_bf16, rot_half=rot_half),
        grid=(m // tm, width // tn), in_specs=in_specs, out_shape=tuple(out_shape), out_specs=tuple(out_specs),
        compiler_params=_cparams(("parallel", "arbitrary")), name=f"proj_{mode}")(*args)


def _diff_lambda(dl_ref, lam_init):
    lv = dl_ref[...]
    s01 = jnp.sum(lv[0:1] * lv[1:2], axis=-1, keepdims=True)
    s23 = jnp.sum(lv[2:3] * lv[3:4], axis=-1, keepdims=True)
    return jnp.exp(s01) - jnp.exp(s23) + lam_init


def _lanes(x, width):
    reps = width // LANE
    return jnp.concatenate([x] * reps, axis=1) if reps > 1 else x


def _diff_finish(acc, l, lam, sub_w, lam_init, rows):
    o = acc[:rows] / l[:rows] - lam * (acc[rows:] / l[rows:])
    ms = jnp.mean(o * o, axis=-1, keepdims=True)
    return o * lax.rsqrt(ms + NORM_EPS) * sub_w * (1.0 - lam_init)


def _attn_prompt_kernel(dl_ref, sub_ref, q_ref, k_ref, v_ref, o_ref, m_sc, l_sc, acc_sc, *,
                        tq, group, hd, vd, scale, lam_init):
    qi = pl.program_id(2)
    rows = group * tq
    m_sc[...] = jnp.full(m_sc.shape, MASK_VALUE, jnp.float32)
    l_sc[...] = jnp.zeros(l_sc.shape, jnp.float32)
    acc_sc[...] = jnp.zeros(acc_sc.shape, jnp.float32)

    def chunk(ki, masked):
        k0 = pl.multiple_of(ki * tq, tq)
        v = v_ref[pl.ds(k0, tq), :]
        for c in range(2):
            qc = jnp.concatenate([q_ref[:, (g * 2 + c) * hd:(g * 2 + c + 1) * hd] for g in range(group)], axis=0)
            s = _dot_nt(qc, k_ref[pl.ds(k0, tq), c * hd:(c + 1) * hd]) * scale
            if masked:
                r = lax.broadcasted_iota(jnp.int32, (tq, tq), 0)
                col = lax.broadcasted_iota(jnp.int32, (tq, tq), 1)
                keep = jnp.concatenate([col <= r] * group, axis=0)
                s = jnp.where(keep, s, MASK_VALUE)
            m_prev = m_sc[c]
            m_new = jnp.maximum(m_prev, jnp.max(s, axis=-1, keepdims=True))
            alpha = jnp.exp(m_prev - m_new)
            p = jnp.exp(s - _lanes(m_new, tq))
            l_sc[c] = alpha * l_sc[c] + jnp.sum(p, axis=-1, keepdims=True)
            acc_sc[c] = _lanes(alpha, vd) * acc_sc[c] + _dot(_bf16(p), v)
            m_sc[c] = m_new

    def body(ki, carry):
        chunk(ki, False)
        return carry

    lax.fori_loop(0, qi, body, 0)
    chunk(qi, True)

    lam = _diff_lambda(dl_ref, lam_init)
    acc = jnp.concatenate([acc_sc[0], acc_sc[1]], axis=0)
    l = _lanes(jnp.concatenate([l_sc[0], l_sc[1]], axis=0), vd)
    o = _diff_finish(acc, l, lam, sub_ref[...], lam_init, rows)
    for g in range(group):
        o_ref[:, g * vd:(g + 1) * vd] = _bf16(o[g * tq:(g + 1) * tq])


def _attn_prompt(q16, k16, v16, dl, sub_w, batch, seq, kvh, group, hd, vd, lam_init):
    tq = _tile(seq, ATT_TQ, LANE)
    nq = seq // tq
    qw = group * 2 * hd
    return pl.pallas_call(
        functools.partial(_attn_prompt_kernel, tq=tq, group=group, hd=hd, vd=vd,
                          scale=hd ** -0.5, lam_init=lam_init),
        grid=(batch, kvh, nq),
        in_specs=[pl.BlockSpec(dl.shape, lambda b, h, qi: (0, 0)),
                  pl.BlockSpec((1, vd), lambda b, h, qi: (0, 0)),
                  pl.BlockSpec((tq, qw), lambda b, h, qi: (b * nq + qi, h)),
                  pl.BlockSpec((seq, 2 * hd), lambda b, h, qi: (b, h)),
                  pl.BlockSpec((seq, vd), lambda b, h, qi: (b, h))],
        out_shape=jax.ShapeDtypeStruct((batch * seq, kvh * group * vd), jnp.bfloat16),
        out_specs=pl.BlockSpec((tq, group * vd), lambda b, h, qi: (b * nq + qi, h)),
        scratch_shapes=[pltpu.VMEM((2, group * tq, LANE), jnp.float32),
                        pltpu.VMEM((2, group * tq, LANE), jnp.float32),
                        pltpu.VMEM((2, group * tq, vd), jnp.float32)],
        compiler_params=_cparams(("parallel", "parallel", "arbitrary")),
        name="attn_prompt")(dl, sub_w, q16, k16, v16)


def _attn_sample_kernel(pt_ref, dl_ref, sub_ref, bias_ref, biasn_ref, q_ref, kn_ref, vn_ref, *rest, pps, n_steps,
                        prow, vd, scale, lam_init):
    k_refs, v_refs = rest[:pps], rest[pps:2 * pps]
    o_ref = rest[2 * pps]
    kb_sc, vb_sc, m_sc, l_sc, acc_sc = rest[2 * pps + 1:]
    step_id = pl.program_id(1)
    half = q_ref.shape[2]

    @pl.when(step_id == 0)
    def _():
        m_sc[...] = jnp.full(m_sc.shape, MASK_VALUE, jnp.float32)
        l_sc[...] = jnp.zeros(l_sc.shape, jnp.float32)
        acc_sc[...] = jnp.zeros(acc_sc.shape, jnp.float32)

    def attend(k_of_map, vmat, bias):
        s = jnp.concatenate([_dot_nt(q_ref[0, c], k_of_map(c)) for c in range(2)], axis=0) * scale + bias
        m_prev = m_sc[...]
        m_new = jnp.maximum(m_prev, jnp.max(s, axis=-1, keepdims=True))
        alpha = jnp.exp(m_prev - m_new)
        p = jnp.exp(s - _lanes(m_new, s.shape[1]))
        l_sc[...] = alpha * l_sc[...] + jnp.sum(p, axis=-1, keepdims=True)
        m_sc[...] = m_new
        acc_sc[...] = _lanes(alpha, vd) * acc_sc[...] + _dot(_bf16(p), vmat)

    for p in range(pps):
        for c in range(2):
            kb_sc[c, p * prow:(p + 1) * prow, :] = _bf16(k_refs[p][pl.ds(c, prow, stride=2), :])
        vb_sc[p * prow:(p + 1) * prow, :] = _bf16(v_refs[p][...])
    page_bias = bias_ref[...]
    attend(lambda c: kb_sc[c], vb_sc[...], jnp.concatenate([page_bias] * pps, axis=1) if pps > 1 else page_bias)

    @pl.when(step_id == n_steps - 1)
    def _():
        attend(lambda c: kn_ref[0, c], vn_ref[0], biasn_ref[...])
        lam = _diff_lambda(dl_ref, lam_init)
        o_ref[0] = _diff_finish(acc_sc[...], _lanes(l_sc[...], vd), lam, sub_ref[...], lam_init, half)


def _sample_biases(kvh, group, dec_seq, page, new_rows):
    import numpy as np
    rq = group * dec_seq
    r = np.arange(2 * kvh * rq) % (kvh * rq)
    row_h, row_t = (r // rq)[:, None], (r % dec_seq)[:, None]
    jp = np.arange(page * kvh)[None, :]
    jn = np.arange(new_rows)[None, :]
    page_keep = (jp % kvh) == row_h
    new_keep = ((jn % kvh) == row_h) & ((jn // kvh) <= row_t) & ((jn // kvh) < dec_seq)
    to_bias = lambda keep: jnp.asarray(np.where(keep, 0.0, MASK_VALUE), jnp.float32)
    return to_bias(page_keep), to_bias(new_keep)


def _attn_sample(q16, kn16, vn16, cache_k, cache_v, layer, page_table, dl, sub_w, group, dec_seq, lam_init):
    depth, n_pool, page, kvh, _, hd = cache_k.shape
    vd = cache_v.shape[-1]
    dec_b, n_pages = page_table.shape
    half = kvh * group * dec_seq
    prow = page * kvh
    pps = math.gcd(n_pages, PAGES_PER_STEP)
    n_steps = n_pages // pps
    ck = cache_k.reshape(depth * n_pool * 2 * prow, hd)
    cv = cache_v.reshape(depth * n_pool * prow, vd)
    page0 = layer * n_pool
    bias_page, bias_new = _sample_biases(kvh, group, dec_seq, page, kn16.shape[2])

    def page_map(p):
        return lambda b, s, pt: (page0 + pt[b * n_pages + s * pps + p], 0)

    const2 = lambda b, s, pt: (0, 0)
    in_specs = [pl.BlockSpec(dl.shape, const2),
                pl.BlockSpec((1, vd), const2),
                pl.BlockSpec(bias_page.shape, const2),
                pl.BlockSpec(bias_new.shape, const2),
                pl.BlockSpec((1,) + q16.shape[1:], lambda b, s, pt: (b, 0, 0, 0)),
                pl.BlockSpec((1,) + kn16.shape[1:], lambda b, s, pt: (b, 0, 0, 0)),
                pl.BlockSpec((1,) + vn16.shape[1:], lambda b, s, pt: (b, 0, 0))]
    in_specs += [pl.BlockSpec((2 * prow, hd), page_map(p)) for p in range(pps)]
    in_specs += [pl.BlockSpec((prow, vd), page_map(p)) for p in range(pps)]
    grid_spec = pltpu.PrefetchScalarGridSpec(
        num_scalar_prefetch=1, grid=(dec_b, n_steps), in_specs=in_specs,
        out_specs=pl.BlockSpec((1, half, vd), lambda b, s, pt: (b, 0, 0)),
        scratch_shapes=[pltpu.VMEM((2, pps * prow, hd), jnp.bfloat16),
                        pltpu.VMEM((pps * prow, vd), jnp.bfloat16),
                        pltpu.VMEM((2 * half, LANE), jnp.float32),
                        pltpu.VMEM((2 * half, LANE), jnp.float32),
                        pltpu.VMEM((2 * half, vd), jnp.float32)])
    return pl.pallas_call(
        functools.partial(_attn_sample_kernel, pps=pps, n_steps=n_steps, prow=prow, vd=vd, scale=hd ** -0.5,
                          lam_init=lam_init),
        grid_spec=grid_spec,
        out_shape=jax.ShapeDtypeStruct((dec_b, half, vd), jnp.float32),
        compiler_params=_cparams(("parallel", "arbitrary")),
        name="attn_sample")(page_table.reshape(-1), dl, sub_w, bias_page, bias_new, q16, kn16, vn16,
                            *([ck] * pps), *([cv] * pps))


def _lru_gates(xc, gw_ref, gb_ref, lam_ref):
    blk = gw_ref.shape[-1]
    a_parts, u_parts = [], []
    sp = jax.nn.softplus(-lam_ref[...])
    for hd in range(xc.shape[1] // blk):
        sl = slice(hd * blk, (hd + 1) * blk)
        xh = xc[:, sl]
        xh16 = _bf16(xh)
        i_gate = jax.nn.sigmoid(_dot(xh16, gw_ref[0, hd]) + gb_ref[0:1, sl])
        r_gate = jax.nn.sigmoid(_dot(xh16, gw_ref[1, hd]) + gb_ref[1:2, sl])
        a = jnp.exp(-LRU_C * r_gate * sp[:, sl])
        a_parts.append(a)
        u_parts.append(jnp.sqrt(1.0 - a * a) * (i_gate * xh))
    cat = lambda ps: jnp.concatenate(ps, axis=1) if len(ps) > 1 else ps[0]
    return cat(a_parts), cat(u_parts)


def _lru_prompt_kernel(x_ref, yg_ref, cw_ref, cb_ref, gw_ref, gb_ref, lam_ref, h0_ref, buf_ref,
                       o_ref, ht_ref, ext_sc, a_sc, u_sc, hs_sc, h_sc, *, tt, kconv):
    ti = pl.program_id(2)
    pad = SUBLANE
    nprev = kconv - 1

    @pl.when(ti == 0)
    def _():
        ext_sc[pad - nprev:pad, :] = buf_ref[0]
        h_sc[...] = h0_ref[0]

    ext_sc[pad:pad + tt, :] = x_ref[...]
    xc = cb_ref[...] + cw_ref[nprev:kconv, :] * x_ref[...]
    for s in range(1, kconv):
        xc = xc + cw_ref[nprev - s:kconv - s, :] * ext_sc[pad - s:pad - s + tt, :]
    tail = ext_sc[pad + tt - nprev:pad + tt, :]
    ext_sc[pad - nprev:pad, :] = tail

    a, u = _lru_gates(xc, gw_ref, gb_ref, lam_ref)
    row = lax.broadcasted_iota(jnp.int32, a.shape, 0) % SUBLANE
    s = 1
    while s < SUBLANE:
        keep = row >= s
        a_sh = pltpu.roll(a, s, 0)
        u_sh = pltpu.roll(u, s, 0)
        u = jnp.where(keep, a * u_sh + u, u)
        a = jnp.where(keep, a * a_sh, a)
        s *= 2
    a_sc[...] = a
    u_sc[...] = u

    def body(g, h):
        r0 = pl.multiple_of(g * SUBLANE, SUBLANE)
        hrows = a_sc[pl.ds(r0, SUBLANE), :] * h + u_sc[pl.ds(r0, SUBLANE), :]
        hs_sc[pl.ds(r0, SUBLANE), :] = hrows
        return hrows[SUBLANE - 1:SUBLANE, :]

    h = lax.fori_loop(0, tt // SUBLANE, body, h_sc[...])
    h_sc[...] = h
    ht_ref[0] = h
    o_ref[...] = _bf16(hs_sc[...] * yg_ref[...])


def _lru_prompt(xb, yg, conv_w, conv_b, gw16, gate_b, lam, h0, buf0, batch, seq):
    c = xb.shape[1]
    kconv = conv_w.shape[0]
    blk = gw16.shape[-1]
    cb = _tile(c, LRU_CB, blk)
    tt = _tile(seq, LRU_TT, SUBLANE)
    nt = seq // tt
    chan = lambda b, ci, ti: (0, ci)
    return pl.pallas_call(
        functools.partial(_lru_prompt_kernel, tt=tt, kconv=kconv),
        grid=(batch, c // cb, nt),
        in_specs=[pl.BlockSpec((tt, cb), lambda b, ci, ti: (b * nt + ti, ci)),
                  pl.BlockSpec((tt, cb), lambda b, ci, ti: (b * nt + ti, ci)),
                  pl.BlockSpec((kconv, cb), chan),
                  pl.BlockSpec((1, cb), chan),
                  pl.BlockSpec((2, cb // blk, blk, blk), lambda b, ci, ti: (0, ci, 0, 0)),
                  pl.BlockSpec((2, cb), chan),
                  pl.BlockSpec((1, cb), chan),
                  pl.BlockSpec((1, 1, cb), lambda b, ci, ti: (b, 0, ci)),
                  pl.BlockSpec((1, kconv - 1, cb), lambda b, ci, ti: (b, 0, ci))],
        out_shape=(jax.ShapeDtypeStruct((batch * seq, c), jnp.bfloat16),
                   jax.ShapeDtypeStruct((batch, 1, c), jnp.float32)),
        out_specs=(pl.BlockSpec((tt, cb), lambda b, ci, ti: (b * nt + ti, ci)),
                   pl.BlockSpec((1, 1, cb), lambda b, ci, ti: (b, 0, ci))),
        scratch_shapes=[pltpu.VMEM((tt + SUBLANE, cb), jnp.float32),
                        pltpu.VMEM((tt, cb), jnp.float32),
                        pltpu.VMEM((tt, cb), jnp.float32),
                        pltpu.VMEM((tt, cb), jnp.float32),
                        pltpu.VMEM((1, cb), jnp.float32)],
        compiler_params=_cparams(("parallel", "parallel", "arbitrary")),
        name="lru_prompt")(xb, yg, conv_w, conv_b, gw16, gate_b, lam, h0, buf0)


def _lru_sample_kernel(x_ref, yg_ref, cw_ref, cb_ref, gw_ref, gb_ref, lam_ref, h0_ref, buf_ref,
                       o_ref, ht_ref, *, nb, nsteps, kconv):
    full = jnp.concatenate([buf_ref[...], x_ref[...]], axis=0)
    xc = cb_ref[...] + cw_ref[0:1, :] * full[0:nsteps * nb]
    for j in range(1, kconv):
        xc = xc + cw_ref[j:j + 1, :] * full[j * nb:(j + nsteps) * nb]
    a, u = _lru_gates(xc, gw_ref, gb_ref, lam_ref)
    h = h0_ref[...]
    for t in range(nsteps):
        sl = slice(t * nb, (t + 1) * nb)
        h = a[sl] * h + u[sl]
        o_ref[sl, :] = _bf16(h * yg_ref[sl, :])
    ht_ref[...] = h


def _lru_sample(xb, yg, conv_w, conv_b, gw16, gate_b, lam, h0, buf_tm, nb, nsteps):
    c = xb.shape[1]
    kconv = conv_w.shape[0]
    blk = gw16.shape[-1]
    cb = _tile(c, LRU_CB, blk)
    chan = lambda ci: (0, ci)
    return pl.pallas_call(
        functools.partial(_lru_sample_kernel, nb=nb, nsteps=nsteps, kconv=kconv),
        grid=(c // cb,),
        in_specs=[pl.BlockSpec((nsteps * nb, cb), chan),
                  pl.BlockSpec((nsteps * nb, cb), chan),
                  pl.BlockSpec((kconv, cb), chan),
                  pl.BlockSpec((1, cb), chan),
                  pl.BlockSpec((2, cb // blk, blk, blk), lambda ci: (0, ci, 0, 0)),
                  pl.BlockSpec((2, cb), chan),
                  pl.BlockSpec((1, cb), chan),
                  pl.BlockSpec((nb, cb), chan),
                  pl.BlockSpec(((kconv - 1) * nb, cb), chan)],
        out_shape=(jax.ShapeDtypeStruct((nsteps * nb, c), jnp.bfloat16),
                   jax.ShapeDtypeStruct((nb, c), jnp.float32)),
        out_specs=(pl.BlockSpec((nsteps * nb, cb), chan), pl.BlockSpec((nb, cb), chan)),
        compiler_params=_cparams(("parallel",)),
        name="lru_sample")(xb, yg, conv_w, conv_b, gw16, gate_b, lam, h0, buf_tm)


def _merge_kernel(a_ref, b_ref, w1_ref, w2_ref, g1_ref, g2_ref, o_ref):
    r1 = _dot(a_ref[...], w1_ref[...])
    r2 = _dot(b_ref[...], w2_ref[...])
    o_ref[...] = _bf16(g1_ref[...].astype(jnp.float32) * r1 + g2_ref[...].astype(jnp.float32) * r2)


def _merge(o_att16, o_lru16, wb16, g16):
    m, ka = o_att16.shape
    kb = o_lru16.shape[1]
    n = wb16.shape[1]
    tm = _tile(m, MERGE_TM, 16)
    tn = _tile(n, MERGE_TN, LANE)
    assert ka % kb == 0
    g2_off = n // tn
    return pl.pallas_call(
        _merge_kernel, grid=(m // tm, n // tn),
        in_specs=[pl.BlockSpec((tm, ka), lambda i, j: (i, 0)),
                  pl.BlockSpec((tm, kb), lambda i, j: (i, 0)),
                  pl.BlockSpec((ka, tn), lambda i, j: (0, j)),
                  pl.BlockSpec((kb, tn), lambda i, j: (ka // kb, j)),
                  pl.BlockSpec((tm, tn), lambda i, j: (i, j)),
                  pl.BlockSpec((tm, tn), lambda i, j: (i, j + g2_off))],
        out_shape=jax.ShapeDtypeStruct((m, n), jnp.bfloat16),
        out_specs=pl.BlockSpec((tm, tn), lambda i, j: (i, j)),
        compiler_params=_cparams(("parallel", "arbitrary")),
        name="merge")(o_att16, o_lru16, wb16, wb16, g16, g16)


def _mm_ln_kernel(a_ref, w_ref, res_ref, g_ref, b_ref, *rest, nk, alpha, want_bf16):
    o_ref = rest[0]
    k = pl.program_id(1)
    tm, n = o_ref.shape

    @pl.when(k == 0)
    def _():
        o_ref[...] = alpha * res_ref[...]

    for n0 in range(0, n, LN_NCHUNK):
        sl = slice(n0, min(n0 + LN_NCHUNK, n))
        o_ref[:, sl] += _dot(a_ref[...], w_ref[:, sl])

    @pl.when(k == nk - 1)
    def _():
        for r0 in range(0, tm, LN_RCHUNK):
            rs = slice(r0, min(r0 + LN_RCHUNK, tm))
            y = o_ref[rs, :]
            mu = jnp.mean(y, axis=-1, keepdims=True)
            yc = y - mu
            var = jnp.mean(yc * yc, axis=-1, keepdims=True)
            out = yc * lax.rsqrt(var + NORM_EPS) * g_ref[...] + b_ref[...]
            o_ref[rs, :] = out
            if want_bf16:
                rest[1][rs, :] = _bf16(out)


def _mm_ln(a16, w16, res, gain, bias, alpha, want_bf16):
    m, kdim = a16.shape
    n = w16.shape[1]
    tm = _tile(m, LN_TM, 16)
    fixed = tm * n * (2 * 4 + (2 * 2 if want_bf16 else 0) + 4) + 2 * tm * LN_NCHUNK * 4
    tk = _tile(kdim, LN_TK, LANE)
    while tk * 2 <= kdim and kdim % (tk * 2) == 0 and fixed + 2 * (tk * 2) * (n + tm) * 2 <= LN_VMEM_BUDGET:
        tk *= 2
    nk = kdim // tk
    row = pl.BlockSpec((tm, n), lambda i, k: (i, 0))
    vec = pl.BlockSpec((1, n), lambda i, k: (0, 0))
    out_shape = [jax.ShapeDtypeStruct((m, n), jnp.float32)]
    out_specs = [row]
    if want_bf16:
        out_shape.append(jax.ShapeDtypeStruct((m, n), jnp.bfloat16))
        out_specs.append(row)
    return pl.pallas_call(
        functools.partial(_mm_ln_kernel, nk=nk, alpha=alpha, want_bf16=want_bf16),
        grid=(m // tm, nk),
        in_specs=[pl.BlockSpec((tm, tk), lambda i, k: (i, k)),
                  pl.BlockSpec((tk, n), lambda i, k: (k, 0)),
                  pl.BlockSpec((tm, n), lambda i, k: (i, 0), pipeline_mode=pl.Buffered(1)),
                  vec, vec],
        out_shape=tuple(out_shape), out_specs=tuple(out_specs),
        compiler_params=_cparams(("parallel", "arbitrary")),
        name="mm_ln")(a16, w16, res, gain, bias)


def _ffn_up_prompt_kernel(x_ref, wu_ref, wp_ref, cw_ref, cb_ref, buf_ref, o_ref, tail_ref, ext_sc, *,
                          tm, tiles_per_seq, kconv):
    i = pl.program_id(1)
    pad = SUBLANE
    nprev = kconv - 1

    @pl.when(i % tiles_per_seq == 0)
    def _():
        ext_sc[pad - nprev:pad, :] = buf_ref[0]

    x = x_ref[...]
    u = _dot(x, wu_ref[...])
    up = _dot(x, wp_ref[...])
    ext_sc[pad:pad + tm, :] = u
    uc = cb_ref[...] + cw_ref[nprev:kconv, :] * u
    for s in range(1, kconv):
        uc = uc + cw_ref[nprev - s:kconv - s, :] * ext_sc[pad - s:pad - s + tm, :]
    tail = ext_sc[pad + tm - nprev:pad + tm, :]
    ext_sc[pad - nprev:pad, :] = tail
    tail_ref[0] = ext_sc[tm:tm + pad, :]
    o_ref[...] = _bf16(jax.nn.silu(uc) * up)


def _ffn_up_prompt(y16, wu16, wp16, conv_w, conv_b, buf0, batch, seq):
    m, kdim = y16.shape
    n = wu16.shape[1]
    kconv = conv_w.shape[0]
    tm = _tile(seq, FFN_TM, 16)
    tn = _tile(n, FFN_TN, LANE)
    tps = seq // tm
    col = lambda j, i: (0, j)
    return pl.pallas_call(
        functools.partial(_ffn_up_prompt_kernel, tm=tm, tiles_per_seq=tps, kconv=kconv),
        grid=(n // tn, m // tm),
        in_specs=[pl.BlockSpec((tm, kdim), lambda j, i: (i, 0)),
                  pl.BlockSpec((kdim, tn), col),
                  pl.BlockSpec((kdim, tn), col),
                  pl.BlockSpec((kconv, tn), col),
                  pl.BlockSpec((1, tn), col),
                  pl.BlockSpec((1, kconv - 1, tn), lambda j, i: (i // tps, 0, j))],
        out_shape=(jax.ShapeDtypeStruct((m, n), jnp.bfloat16),
                   jax.ShapeDtypeStruct((batch, SUBLANE, n), jnp.float32)),
        out_specs=(pl.BlockSpec((tm, tn), lambda j, i: (i, j)),
                   pl.BlockSpec((1, SUBLANE, tn), lambda j, i: (i // tps, 0, j))),
        scratch_shapes=[pltpu.VMEM((tm + SUBLANE, tn), jnp.float32)],
        compiler_params=_cparams(("parallel", "arbitrary")),
        name="ffn_up_prompt")(y16, wu16, wp16, conv_w, conv_b, buf0)


def _ffn_up_sample_kernel(x_ref, wu_ref, wp_ref, cw_ref, cb_ref, buf_ref, o_ref, tail_ref, *, nb, nsteps, kconv):
    x = x_ref[...]
    u = _dot(x, wu_ref[...])
    up = _dot(x, wp_ref[...])
    full = jnp.concatenate([buf_ref[...], u], axis=0)
    uc = cb_ref[...] + cw_ref[0:1, :] * full[0:nsteps * nb]
    for j in range(1, kconv):
        uc = uc + cw_ref[j:j + 1, :] * full[j * nb:(j + nsteps) * nb]
    tail_ref[...] = full[nsteps * nb:(nsteps + kconv - 1) * nb]
    o_ref[...] = _bf16(jax.nn.silu(uc) * up)


def _ffn_up_sample(y16, wu16, wp16, conv_w, conv_b, buf_tm, nb, nsteps):
    m, kdim = y16.shape
    n = wu16.shape[1]
    kconv = conv_w.shape[0]
    tn = _tile(n, FFN_TN, LANE)
    col = lambda j: (0, j)
    return pl.pallas_call(
        functools.partial(_ffn_up_sample_kernel, nb=nb, nsteps=nsteps, kconv=kconv),
        grid=(n // tn,),
        in_specs=[pl.BlockSpec((m, kdim), lambda j: (0, 0)),
                  pl.BlockSpec((kdim, tn), col),
                  pl.BlockSpec((kdim, tn), col),
                  pl.BlockSpec((kconv, tn), col),
                  pl.BlockSpec((1, tn), col),
                  pl.BlockSpec(((kconv - 1) * nb, tn), col)],
        out_shape=(jax.ShapeDtypeStruct((m, n), jnp.bfloat16),
                   jax.ShapeDtypeStruct(((kconv - 1) * nb, n), jnp.float32)),
        out_specs=(pl.BlockSpec((m, tn), col), pl.BlockSpec(((kconv - 1) * nb, tn), col)),
        compiler_params=_cparams(("parallel",)),
        name="ffn_up_sample")(y16, wu16, wp16, conv_w, conv_b, buf_tm)


def _pad_cols(a, n):
    return jnp.pad(a, [(0, 0)] * (a.ndim - 1) + [(0, n - a.shape[-1])])


def _cast_pad_kernel(x_ref, o_ref, *, nrb, ncb):
    valid = (pl.program_id(0) < nrb) & (pl.program_id(1) < ncb)

    @pl.when(valid)
    def _():
        o_ref[...] = _bf16(x_ref[0])

    @pl.when(jnp.logical_not(valid))
    def _():
        o_ref[...] = jnp.zeros(o_ref.shape, o_ref.dtype)


def _cast_pad(w, layer, col_off, width, rows_out, cols_out):
    rows = w.shape[1]
    rb = _tile(math.gcd(rows, rows_out), 1024, 16)
    cb = _tile(math.gcd(math.gcd(width, cols_out), col_off) if col_off else math.gcd(width, cols_out), 1024, LANE)
    nrb, ncb, cob = rows // rb, width // cb, col_off // cb
    return pl.pallas_call(
        functools.partial(_cast_pad_kernel, nrb=nrb, ncb=ncb),
        grid=(rows_out // rb, cols_out // cb),
        in_specs=[pl.BlockSpec((1, rb, cb),
                               lambda i, j: (layer, jnp.minimum(i, nrb - 1), jnp.minimum(j, ncb - 1) + cob))],
        out_shape=jax.ShapeDtypeStruct((rows_out, cols_out), jnp.bfloat16),
        out_specs=pl.BlockSpec((rb, cb), lambda i, j: (i, j)),
        compiler_params=_cparams(("parallel", "parallel")),
        name="cast_pad")(w)


def kernel(x_prompt, x_sample, cache_k, cache_v, page_table, state_lru_h, state_lru_conv, state_ffn_conv,
           w_in, diff_lambda, subln_w, lru_conv_w, lru_conv_b, lru_gate_w, lru_gate_b, lru_lambda,
           w_branch, w_out, ln1_g, ln1_b, ffn_w_in, ffn_conv_w, ffn_conv_b, ffn_w_down, ln2_g, ln2_b):
    depth, d_model, in_w = w_in.shape
    bp, tp, _ = x_prompt.shape
    bs, ts, _ = x_sample.shape
    n_pool, page, kvh, _, hd = cache_k.shape[1:]
    vd = cache_v.shape[-1]
    lru_w = lru_lambda.shape[-1]
    lru_heads, lru_blk = lru_gate_w.shape[2], lru_gate_w.shape[3]
    d_ff = ffn_conv_w.shape[-1]
    k_w, v_w = kvh * 2 * hd, kvh * vd
    q_w = in_w - k_w - v_w - 2 * lru_w - 2 * d_model
    n_heads = q_w // (2 * hd)
    group = n_heads // kvh
    att_out_w = n_heads * vd
    rot_dim = hd // 4
    past = page_table.shape[1] * page
    alpha = (2.0 * depth) ** 0.25
    o_k, o_v, o_xb, o_yb, o_g = q_w, q_w + k_w, q_w + k_w + v_w, q_w + k_w + v_w + lru_w, q_w + k_w + v_w + 2 * lru_w
    ff_align = max(FFN_TN, LN_TK)
    d_ff_pad = -(-d_ff // ff_align) * ff_align

    mp, ms = bp * tp, bs * ts
    tab_rows = _tile(tp, PROJ_TM, 16)
    tab_p = _rope_tables(tp // tab_rows, tab_rows, rot_dim, 0, True)
    tab_s = _rope_tables(ts, bs, rot_dim, past, False)

    yp = x_prompt.reshape(mp, d_model)
    ys = x_sample.transpose(1, 0, 2).reshape(ms, d_model)
    outs_p, outs_s = [], []
    for l in range(depth):
        lam_init = 0.8 - 0.6 * math.exp(-0.3 * l)
        w_in16 = _bf16(w_in[l])
        wb16 = _bf16(w_branch[l])
        wo16 = _bf16(w_out[l])
        wu16 = _cast_pad(ffn_w_in, l, 0, d_ff, d_model, d_ff_pad)
        wp16 = _cast_pad(ffn_w_in, l, d_ff, d_ff, d_model, d_ff_pad)
        wd16 = _cast_pad(ffn_w_down, l, 0, d_model, d_ff_pad, d_model)
        fcw = _pad_cols(ffn_conv_w[l], d_ff_pad)
        fcb = _pad_cols(ffn_conv_b[l][None], d_ff_pad)
        gw16 = _bf16(lru_gate_w[l])
        gate_b = lru_gate_b[l].reshape(2, lru_w)
        dl = diff_lambda[l]
        sub_w = subln_w[l][None]
        common = dict(conv_w=lru_conv_w[l], conv_b=lru_conv_b[l][None], gw16=gw16, gate_b=gate_b,
                      lam=lru_lambda[l][None])

        def project(x16, tables):
            q16, = _proj(x16, w_in16, 0, q_w, "rope", False, True, tables, rot_dim // 2)
            k32, k16 = _proj(x16, w_in16, o_k, k_w, "rope", True, True, tables, rot_dim // 2)
            v32, v16 = _proj(x16, w_in16, o_v, v_w, "plain", True, True)
            xb, = _proj(x16, w_in16, o_xb, lru_w, "plain", True, False)
            yg, = _proj(x16, w_in16, o_yb, lru_w, "gelu", True, False)
            g16, = _proj(x16, w_in16, o_g, 2 * d_model, "sigmoid", False, True)
            return q16, k32, k16, v32, v16, xb, yg, g16

        def finish(x32, o_att16, o_lru16, g16, ffn_up):
            merged16 = _merge(o_att16, o_lru16, wb16, g16)
            y1, y1_16 = _mm_ln(merged16, wo16, x32, ln1_g[l][None], ln1_b[l][None], alpha, True)
            f16, tail = ffn_up(y1_16)
            y2, = _mm_ln(f16, wd16, y1, ln2_g[l][None], ln2_b[l][None], alpha, False)
            return y2, tail

        q16, k32, k16, v32, v16, xb, yg, g16 = project(_bf16(yp), tab_p)
        o_att16 = _attn_prompt(q16, k16, v16, dl, sub_w, bp, tp, kvh, group, hd, vd, lam_init)
        zeros = lambda *s: jnp.zeros(s, jnp.float32)
        o_lru16, ht = _lru_prompt(xb, yg, h0=zeros(bp, 1, lru_w), buf0=zeros(bp, lru_conv_w.shape[1] - 1, lru_w),
                                  batch=bp, seq=tp, **common)
        kf = ffn_conv_w.shape[1]
        yp, tail = finish(yp, o_att16, o_lru16, g16,
                          lambda a: _ffn_up_prompt(a, wu16, wp16, fcw, fcb, zeros(bp, kf - 1, d_ff_pad), bp, tp))
        kl = lru_conv_w.shape[1]
        outs_p.append((k32.reshape(bp, tp, kvh, 2, hd), v32.reshape(bp, tp, kvh, vd), ht.reshape(bp, lru_w),
                       xb.reshape(bp, tp, lru_w)[:, tp - (kl - 1):],
                       tail[:, SUBLANE - (kf - 1):, :d_ff]))

        q16, k32, k16, v32, v16, xb, yg, g16 = project(_bf16(ys), tab_s)
        qs = q16.reshape(ts, bs, kvh, group, 2, hd).transpose(1, 4, 2, 3, 0, 5).reshape(bs, 2, kvh * group * ts, hd)
        new_rows = -(-ts * kvh // LANE) * LANE
        kn = k16.reshape(ts, bs, kvh, 2, hd).transpose(1, 3, 0, 2, 4).reshape(bs, 2, ts * kvh, hd)
        kn = jnp.pad(kn, ((0, 0), (0, 0), (0, new_rows - ts * kvh), (0, 0)))
        vn = v16.reshape(ts, bs, kvh, vd).transpose(1, 0, 2, 3).reshape(bs, ts * kvh, vd)
        vn = jnp.pad(vn, ((0, 0), (0, new_rows - ts * kvh), (0, 0)))
        o_s = _attn_sample(qs, kn, vn, cache_k, cache_v, l, page_table, dl, sub_w, group, ts, lam_init)
        o_att16 = _bf16(o_s.reshape(bs, kvh, group, ts, vd).transpose(3, 0, 1, 2, 4).reshape(ms, att_out_w))
        to_tm = lambda a: a.transpose(1, 0, 2).reshape(-1, a.shape[-1])
        o_lru16, ht = _lru_sample(xb, yg, h0=state_lru_h[l], buf_tm=to_tm(state_lru_conv[l]), nb=bs, nsteps=ts,
                                  **common)
        fbuf = _pad_cols(to_tm(state_ffn_conv[l]), d_ff_pad)
        ys, tail = finish(ys, o_att16, o_lru16, g16,
                          lambda a: _ffn_up_sample(a, wu16, wp16, fcw, fcb, fbuf, bs, ts))
        from_tm = lambda a, n: a.reshape(n, bs, -1).transpose(1, 0, 2)
        lru_full = jnp.concatenate([to_tm(state_lru_conv[l]), xb], axis=0)
        outs_s.append((from_tm(k32, ts).reshape(bs, ts, kvh, 2, hd), from_tm(v32, ts).reshape(bs, ts, kvh, vd), ht,
                       from_tm(lru_full[lru_full.shape[0] - (kl - 1) * bs:], kl - 1),
                       from_tm(tail, kf - 1)[:, :, :d_ff]))

    y_prompt = yp.reshape(bp, tp, d_model)
    y_sample = ys.reshape(ts, bs, d_model).transpose(1, 0, 2)
    stack = lambda outs, i: jnp.stack([o[i] for o in outs])
    return (y_prompt, y_sample,
            stack(outs_p, 0), stack(outs_p, 1), stack(outs_p, 2), stack(outs_p, 3), stack(outs_p, 4),
            stack(outs_s, 0), stack(outs_s, 1), stack(outs_s, 2), stack(outs_s, 3), stack(outs_s, 4))
```

```python
import functools
import math

import jax
import jax.numpy as jnp
from jax import lax
from jax.experimental import pallas as pl
from jax.experimental.pallas import tpu as pltpu

ROPE_THETA = 500000.0
LRU_C = 8.0
NORM_EPS = 1e-5
LANE = 128
SUBLANE = 8
VMEM_LIMIT_BYTES = 56 * 1024 * 1024
MASK_VALUE = -1e30
PROJ_TM, PROJ_TN = 1024, 512
ATT_TQ = 512
PAGES_PER_STEP = 8
LRU_TT, LRU_CB = 512, 512
MERGE_TM, MERGE_TN = 1024, 512
LN_TM, LN_TK = 512, 512
LN_NCHUNK, LN_RCHUNK = 512, 128
LN_VMEM_BUDGET = 47 * 1024 * 1024
FFN_TM, FFN_TN = 1024, 512
CAST_BLOCK_ELEMS = 1024 * 1024


def _cparams(sem):
    return pltpu.CompilerParams(dimension_semantics=sem, vmem_limit_bytes=VMEM_LIMIT_BYTES)


def _tile(n, pref, align):
    if n <= pref:
        return n
    t = (pref // align) * align
    while t >= align:
        if n % t == 0:
            return t
        t -= align
    raise ValueError(f"no tile for {n} (pref {pref}, align {align})")


def _bf16(x):
    return x.astype(jnp.bfloat16)


def _dot(a, b):
    return jnp.dot(a, b, preferred_element_type=jnp.float32)


def _dot_nt(a, b):
    return lax.dot_general(a, b, (((1,), (1,)), ((), ())), preferred_element_type=jnp.float32)


def _rope_table_kernel(a_ref, b_ref, c_ref, *, rot_dim, pos_base, pos_per_step):
    rows = a_ref.shape[0]
    half = rot_dim // 2
    lane = lax.broadcasted_iota(jnp.int32, (rows, LANE), 1)
    if pos_per_step:
        row = lax.broadcasted_iota(jnp.int32, (rows, LANE), 0)
        pos = pl.program_id(0) * rows + row
    else:
        pos = jnp.full((rows, LANE), pos_base, jnp.int32) + pl.program_id(0)
    fi = jnp.where(lane < half, lane, lane - half).astype(jnp.float32)
    inv_freq = jnp.exp(fi * (-(2.0 / rot_dim) * math.log(ROPE_THETA)))
    ang = pos.astype(jnp.float32) * inv_freq
    cos, sin = jnp.cos(ang), jnp.sin(ang)
    a_ref[...] = jnp.where(lane < rot_dim, cos, 1.0)
    b_ref[...] = jnp.where(lane < half, -sin, 0.0)
    c_ref[...] = jnp.where((lane >= half) & (lane < rot_dim), sin, 0.0)


def _rope_tables(n_steps, rows, rot_dim, pos_base, pos_per_step):
    shp = jax.ShapeDtypeStruct((n_steps * rows, LANE), jnp.float32)
    spec = pl.BlockSpec((rows, LANE), lambda s: (s, 0))
    return pl.pallas_call(
        functools.partial(_rope_table_kernel, rot_dim=rot_dim, pos_base=pos_base, pos_per_step=pos_per_step),
        grid=(n_steps,), out_shape=(shp, shp, shp), out_specs=(spec, spec, spec),
        compiler_params=_cparams(("arbitrary",)), name="rope_tables")()


def _gelu_tanh(x):
    return 0.5 * x * (1.0 + jnp.tanh(math.sqrt(2.0 / math.pi) * (x + 0.044715 * (x * x * x))))


def _proj_kernel(x_ref, w_ref, *rest, mode, want_f32, want_bf16, rot_half, out_scale):
    acc = _dot(x_ref[...], w_ref[...])
    if out_scale != 1.0:
        acc = acc * out_scale
    outs = rest[3:] if mode == "rope" else rest

    def emit(res, sl):
        i = 0
        if want_f32:
            outs[i][:, sl] = res
            i += 1
        if want_bf16:
            outs[i][:, sl] = _bf16(res)

    if mode == "rope":
        ta, tb, tc = rest[0][...], rest[1][...], rest[2][...]
        for c in range(acc.shape[1] // LANE):
            sl = slice(c * LANE, (c + 1) * LANE)
            xa = acc[:, sl]
            emit(xa * ta + pltpu.roll(xa, LANE - rot_half, 1) * tb + pltpu.roll(xa, rot_half, 1) * tc, sl)
    elif mode == "gelu":
        emit(_gelu_tanh(acc), slice(None))
    elif mode == "sigmoid":
        emit(jax.nn.sigmoid(acc), slice(None))
    else:
        emit(acc, slice(None))


def _proj(x16, w16, col_off, width, mode, want_f32, want_bf16, tables=None, rot_half=0, out_scale=1.0):
    m, kdim = x16.shape
    tm = _tile(m, PROJ_TM, 16)
    tn_pref = PROJ_TN if (want_f32 and want_bf16) else 2 * PROJ_TN
    tn = _tile(math.gcd(width, col_off) if col_off else width, tn_pref, LANE)
    off = col_off // tn
    in_specs = [pl.BlockSpec((tm, kdim), lambda i, j: (i, 0)),
                pl.BlockSpec((kdim, tn), lambda i, j: (0, j + off))]
    args = [x16, w16]
    if mode == "rope":
        period = tables[0].shape[0] // tm
        tspec = pl.BlockSpec((tm, LANE), lambda i, j: (i % period, 0))
        in_specs += [tspec, tspec, tspec]
        args += list(tables)
    out_shape, out_specs = [], []
    ospec = pl.BlockSpec((tm, tn), lambda i, j: (i, j))
    if want_f32:
        out_shape.append(jax.ShapeDtypeStruct((m, width), jnp.float32))
        out_specs.append(ospec)
    if want_bf16:
        out_shape.append(jax.ShapeDtypeStruct((m, width), jnp.bfloat16))
        out_specs.append(ospec)
    return pl.pallas_call(
        functools.partial(_proj_kernel, mode=mode, want_f32=want_f32, want_bf16=want_bf16, rot_half=rot_half,
                          out_scale=out_scale),
        grid=(m // tm, width // tn), in_specs=in_specs, out_shape=tuple(out_shape), out_specs=tuple(out_specs),
        compiler_params=_cparams(("parallel", "arbitrary")), name=f"proj_{mode}")(*args)


def _diff_lambda(dl_ref, lam_init):
    lv = dl_ref[...]
    s01 = jnp.sum(lv[0:1] * lv[1:2], axis=-1, keepdims=True)
    s23 = jnp.sum(lv[2:3] * lv[3:4], axis=-1, keepdims=True)
    return jnp.exp(s01) - jnp.exp(s23) + lam_init


def _lanes(x, width):
    reps = width // LANE
    return jnp.concatenate([x] * reps, axis=1) if reps > 1 else x


def _diff_finish(acc, l, lam, sub_w, lam_init, rows):
    o = acc[:rows] / l[:rows] - lam * (acc[rows:] / l[rows:])
    ms = jnp.mean(o * o, axis=-1, keepdims=True)
    return o * lax.rsqrt(ms + NORM_EPS) * sub_w * (1.0 - lam_init)


def _attn_prompt_kernel(dl_ref, sub_ref, q_ref, k_ref, v_ref, o_ref, m_sc, l_sc, acc_sc, *,
                        tq, group, hd, vd, lam_init):
    qi = pl.program_id(2)
    rows = group * tq
    m_sc[...] = jnp.full(m_sc.shape, MASK_VALUE, jnp.float32)
    l_sc[...] = jnp.zeros(l_sc.shape, jnp.float32)
    acc_sc[...] = jnp.zeros(acc_sc.shape, jnp.float32)

    def chunk(ki, masked):
        k0 = pl.multiple_of(ki * tq, tq)
        v = v_ref[pl.ds(k0, tq), :]
        for c in range(2):
            qc = jnp.concatenate([q_ref[:, (g * 2 + c) * hd:(g * 2 + c + 1) * hd] for g in range(group)], axis=0)
            s = _dot_nt(qc, k_ref[pl.ds(k0, tq), c * hd:(c + 1) * hd])
            if masked:
                r = lax.broadcasted_iota(jnp.int32, (tq, tq), 0)
                col = lax.broadcasted_iota(jnp.int32, (tq, tq), 1)
                keep = jnp.concatenate([col <= r] * group, axis=0)
                s = jnp.where(keep, s, MASK_VALUE)
            m_prev = m_sc[c]
            m_new = jnp.maximum(m_prev, jnp.max(s, axis=-1, keepdims=True))
            alpha = jnp.exp(m_prev - m_new)
            p = jnp.exp(s - _lanes(m_new, tq))
            l_sc[c] = alpha * l_sc[c] + jnp.sum(p, axis=-1, keepdims=True)
            acc_sc[c] = _lanes(alpha, vd) * acc_sc[c] + _dot(_bf16(p), v)
            m_sc[c] = m_new

    def pair(pi, carry):
        chunk(2 * pi, False)
        chunk(2 * pi + 1, False)
        return carry

    lax.fori_loop(0, lax.shift_right_logical(qi, 1), pair, 0)

    @pl.when((qi & 1) == 1)
    def _():
        chunk(qi - 1, False)

    chunk(qi, True)

    lam = _diff_lambda(dl_ref, lam_init)
    acc = jnp.concatenate([acc_sc[0], acc_sc[1]], axis=0)
    l = _lanes(jnp.concatenate([l_sc[0], l_sc[1]], axis=0), vd)
    o = _diff_finish(acc, l, lam, sub_ref[...], lam_init, rows)
    for g in range(group):
        o_ref[:, g * vd:(g + 1) * vd] = _bf16(o[g * tq:(g + 1) * tq])


def _attn_prompt(q16, k16, v16, dl, sub_w, batch, seq, kvh, group, hd, vd, lam_init):
    tq = _tile(seq, ATT_TQ, LANE)
    nq = seq // tq
    qw = group * 2 * hd
    return pl.pallas_call(
        functools.partial(_attn_prompt_kernel, tq=tq, group=group, hd=hd, vd=vd, lam_init=lam_init),
        grid=(batch, kvh, nq),
        in_specs=[pl.BlockSpec(dl.shape, lambda b, h, qi: (0, 0)),
                  pl.BlockSpec((1, vd), lambda b, h, qi: (0, 0)),
                  pl.BlockSpec((tq, qw), lambda b, h, qi: (b * nq + qi, h)),
                  pl.BlockSpec((seq, 2 * hd), lambda b, h, qi: (b, h)),
                  pl.BlockSpec((seq, vd), lambda b, h, qi: (b, h))],
        out_shape=jax.ShapeDtypeStruct((batch * seq, kvh * group * vd), jnp.bfloat16),
        out_specs=pl.BlockSpec((tq, group * vd), lambda b, h, qi: (b * nq + qi, h)),
        scratch_shapes=[pltpu.VMEM((2, group * tq, LANE), jnp.float32),
                        pltpu.VMEM((2, group * tq, LANE), jnp.float32),
                        pltpu.VMEM((2, group * tq, vd), jnp.float32)],
        compiler_params=_cparams(("parallel", "parallel", "arbitrary")),
        name="attn_prompt")(dl, sub_w, q16, k16, v16)


def _attn_sample_kernel(pt_ref, dl_ref, sub_ref, bias_ref, biasn_ref, q_ref, kn_ref, vn_ref, *rest, pps, n_steps,
                        prow, vd, lam_init):
    k_refs, v_refs = rest[:pps], rest[pps:2 * pps]
    o_ref = rest[2 * pps]
    kb_sc, vb_sc, m_sc, l_sc, acc_sc = rest[2 * pps + 1:]
    step_id = pl.program_id(1)
    half = q_ref.shape[2]

    @pl.when(step_id == 0)
    def _():
        m_sc[...] = jnp.full(m_sc.shape, MASK_VALUE, jnp.float32)
        l_sc[...] = jnp.zeros(l_sc.shape, jnp.float32)
        acc_sc[...] = jnp.zeros(acc_sc.shape, jnp.float32)

    def attend(k_of_map, vmat, bias):
        s = jnp.concatenate([_dot_nt(q_ref[0, c], k_of_map(c)) for c in range(2)], axis=0) + bias
        m_prev = m_sc[...]
        m_new = jnp.maximum(m_prev, jnp.max(s, axis=-1, keepdims=True))
        alpha = jnp.exp(m_prev - m_new)
        p = jnp.exp(s - _lanes(m_new, s.shape[1]))
        l_sc[...] = alpha * l_sc[...] + jnp.sum(p, axis=-1, keepdims=True)
        m_sc[...] = m_new
        acc_sc[...] = _lanes(alpha, vd) * acc_sc[...] + _dot(_bf16(p), vmat)

    for p in range(pps):
        for c in range(2):
            kb_sc[c, p * prow:(p + 1) * prow, :] = _bf16(k_refs[p][pl.ds(c, prow, stride=2), :])
        vb_sc[p * prow:(p + 1) * prow, :] = _bf16(v_refs[p][...])
    page_bias = bias_ref[...]
    attend(lambda c: kb_sc[c], vb_sc[...], jnp.concatenate([page_bias] * pps, axis=1) if pps > 1 else page_bias)

    @pl.when(step_id == n_steps - 1)
    def _():
        attend(lambda c: kn_ref[0, c], vn_ref[0], biasn_ref[...])
        lam = _diff_lambda(dl_ref, lam_init)
        o_ref[0] = _diff_finish(acc_sc[...], _lanes(l_sc[...], vd), lam, sub_ref[...], lam_init, half)


def _sample_biases(kvh, group, dec_seq, page, new_rows):
    import numpy as np
    rq = group * dec_seq
    r = np.arange(2 * kvh * rq) % (kvh * rq)
    row_h, row_t = (r // rq)[:, None], (r % dec_seq)[:, None]
    jp = np.arange(page * kvh)[None, :]
    jn = np.arange(new_rows)[None, :]
    page_keep = (jp % kvh) == row_h
    new_keep = ((jn % kvh) == row_h) & ((jn // kvh) <= row_t) & ((jn // kvh) < dec_seq)
    to_bias = lambda keep: jnp.asarray(np.where(keep, 0.0, MASK_VALUE), jnp.float32)
    return to_bias(page_keep), to_bias(new_keep)


def _attn_sample(q16, kn16, vn16, cache_k, cache_v, layer, page_table, dl, sub_w, group, dec_seq, lam_init):
    depth, n_pool, page, kvh, _, hd = cache_k.shape
    vd = cache_v.shape[-1]
    dec_b, n_pages = page_table.shape
    half = kvh * group * dec_seq
    prow = page * kvh
    pps = math.gcd(n_pages, PAGES_PER_STEP)
    n_steps = n_pages // pps
    ck = cache_k.reshape(depth * n_pool * 2 * prow, hd)
    cv = cache_v.reshape(depth * n_pool * prow, vd)
    page0 = layer * n_pool
    bias_page, bias_new = _sample_biases(kvh, group, dec_seq, page, kn16.shape[2])

    def page_map(p):
        return lambda b, s, pt: (page0 + pt[b * n_pages + s * pps + p], 0)

    const2 = lambda b, s, pt: (0, 0)
    in_specs = [pl.BlockSpec(dl.shape, const2),
                pl.BlockSpec((1, vd), const2),
                pl.BlockSpec(bias_page.shape, const2),
                pl.BlockSpec(bias_new.shape, const2),
                pl.BlockSpec((1,) + q16.shape[1:], lambda b, s, pt: (b, 0, 0, 0)),
                pl.BlockSpec((1,) + kn16.shape[1:], lambda b, s, pt: (b, 0, 0, 0)),
                pl.BlockSpec((1,) + vn16.shape[1:], lambda b, s, pt: (b, 0, 0))]
    in_specs += [pl.BlockSpec((2 * prow, hd), page_map(p)) for p in range(pps)]
    in_specs += [pl.BlockSpec((prow, vd), page_map(p)) for p in range(pps)]
    grid_spec = pltpu.PrefetchScalarGridSpec(
        num_scalar_prefetch=1, grid=(dec_b, n_steps), in_specs=in_specs,
        out_specs=pl.BlockSpec((1, half, vd), lambda b, s, pt: (b, 0, 0)),
        scratch_shapes=[pltpu.VMEM((2, pps * prow, hd), jnp.bfloat16),
                        pltpu.VMEM((pps * prow, vd), jnp.bfloat16),
                        pltpu.VMEM((2 * half, LANE), jnp.float32),
                        pltpu.VMEM((2 * half, LANE), jnp.float32),
                        pltpu.VMEM((2 * half, vd), jnp.float32)])
    return pl.pallas_call(
        functools.partial(_attn_sample_kernel, pps=pps, n_steps=n_steps, prow=prow, vd=vd, lam_init=lam_init),
        grid_spec=grid_spec,
        out_shape=jax.ShapeDtypeStruct((dec_b, half, vd), jnp.float32),
        compiler_params=_cparams(("parallel", "arbitrary")),
        name="attn_sample")(page_table.reshape(-1), dl, sub_w, bias_page, bias_new, q16, kn16, vn16,
                            *([ck] * pps), *([cv] * pps))


def _lru_gates(xc, gw_ref, gb_ref, lam_ref):
    blk = gw_ref.shape[-1]
    a_parts, u_parts = [], []
    sp = jax.nn.softplus(-lam_ref[...])
    for hd in range(xc.shape[1] // blk):
        sl = slice(hd * blk, (hd + 1) * blk)
        xh = xc[:, sl]
        xh16 = _bf16(xh)
        i_gate = jax.nn.sigmoid(_dot(xh16, gw_ref[0, hd]) + gb_ref[0:1, sl])
        r_gate = jax.nn.sigmoid(_dot(xh16, gw_ref[1, hd]) + gb_ref[1:2, sl])
        a = jnp.exp(-LRU_C * r_gate * sp[:, sl])
        a_parts.append(a)
        u_parts.append(jnp.sqrt(1.0 - a * a) * (i_gate * xh))
    cat = lambda ps: jnp.concatenate(ps, axis=1) if len(ps) > 1 else ps[0]
    return cat(a_parts), cat(u_parts)


def _lru_prompt_kernel(x_ref, yg_ref, cw_ref, cb_ref, gw_ref, gb_ref, lam_ref, h0_ref, buf_ref,
                       o_ref, ht_ref, ext_sc, a_sc, u_sc, hs_sc, h_sc, *, tt, kconv):
    ti = pl.program_id(2)
    pad = SUBLANE
    nprev = kconv - 1

    @pl.when(ti == 0)
    def _():
        ext_sc[pad - nprev:pad, :] = buf_ref[0]
        h_sc[...] = h0_ref[0]

    ext_sc[pad:pad + tt, :] = x_ref[...]
    xc = cb_ref[...] + cw_ref[nprev:kconv, :] * x_ref[...]
    for s in range(1, kconv):
        xc = xc + cw_ref[nprev - s:kconv - s, :] * ext_sc[pad - s:pad - s + tt, :]
    tail = ext_sc[pad + tt - nprev:pad + tt, :]
    ext_sc[pad - nprev:pad, :] = tail

    a, u = _lru_gates(xc, gw_ref, gb_ref, lam_ref)
    row = lax.broadcasted_iota(jnp.int32, a.shape, 0) % SUBLANE
    s = 1
    while s < SUBLANE:
        keep = row >= s
        a_sh = pltpu.roll(a, s, 0)
        u_sh = pltpu.roll(u, s, 0)
        u = jnp.where(keep, a * u_sh + u, u)
        a = jnp.where(keep, a * a_sh, a)
        s *= 2
    a_sc[...] = a
    u_sc[...] = u

    def body(g, h):
        r0 = pl.multiple_of(g * SUBLANE, SUBLANE)
        hrows = a_sc[pl.ds(r0, SUBLANE), :] * h + u_sc[pl.ds(r0, SUBLANE), :]
        hs_sc[pl.ds(r0, SUBLANE), :] = hrows
        return hrows[SUBLANE - 1:SUBLANE, :]

    h = lax.fori_loop(0, tt // SUBLANE, body, h_sc[...])
    h_sc[...] = h
    ht_ref[0] = h
    o_ref[...] = _bf16(hs_sc[...] * yg_ref[...])


def _lru_prompt(xb, yg, conv_w, conv_b, gw16, gate_b, lam, h0, buf0, batch, seq):
    c = xb.shape[1]
    kconv = conv_w.shape[0]
    blk = gw16.shape[-1]
    cb = _tile(c, LRU_CB, blk)
    tt = _tile(seq, LRU_TT, SUBLANE)
    nt = seq // tt
    chan = lambda b, ci, ti: (0, ci)
    return pl.pallas_call(
        functools.partial(_lru_prompt_kernel, tt=tt, kconv=kconv),
        grid=(batch, c // cb, nt),
        in_specs=[pl.BlockSpec((tt, cb), lambda b, ci, ti: (b * nt + ti, ci)),
                  pl.BlockSpec((tt, cb), lambda b, ci, ti: (b * nt + ti, ci)),
                  pl.BlockSpec((kconv, cb), chan),
                  pl.BlockSpec((1, cb), chan),
                  pl.BlockSpec((2, cb // blk, blk, blk), lambda b, ci, ti: (0, ci, 0, 0)),
                  pl.BlockSpec((2, cb), chan),
                  pl.BlockSpec((1, cb), chan),
                  pl.BlockSpec((1, 1, cb), lambda b, ci, ti: (b, 0, ci)),
                  pl.BlockSpec((1, kconv - 1, cb), lambda b, ci, ti: (b, 0, ci))],
        out_shape=(jax.ShapeDtypeStruct((batch * seq, c), jnp.bfloat16),
                   jax.ShapeDtypeStruct((batch, 1, c), jnp.float32)),
        out_specs=(pl.BlockSpec((tt, cb), lambda b, ci, ti: (b * nt + ti, ci)),
                   pl.BlockSpec((1, 1, cb), lambda b, ci, ti: (b, 0, ci))),
        scratch_shapes=[pltpu.VMEM((tt + SUBLANE, cb), jnp.float32),
                        pltpu.VMEM((tt, cb), jnp.float32),
                        pltpu.VMEM((tt, cb), jnp.float32),
                        pltpu.VMEM((tt, cb), jnp.float32),
                        pltpu.VMEM((1, cb), jnp.float32)],
        compiler_params=_cparams(("parallel", "parallel", "arbitrary")),
        name="lru_prompt")(xb, yg, conv_w, conv_b, gw16, gate_b, lam, h0, buf0)


def _lru_sample_kernel(x_ref, yg_ref, cw_ref, cb_ref, gw_ref, gb_ref, lam_ref, h0_ref, buf_ref,
                       o_ref, ht_ref, *, nb, nsteps, kconv):
    full = jnp.concatenate([buf_ref[...], x_ref[...]], axis=0)
    xc = cb_ref[...] + cw_ref[0:1, :] * full[0:nsteps * nb]
    for j in range(1, kconv):
        xc = xc + cw_ref[j:j + 1, :] * full[j * nb:(j + nsteps) * nb]
    a, u = _lru_gates(xc, gw_ref, gb_ref, lam_ref)
    h = h0_ref[...]
    for t in range(nsteps):
        sl = slice(t * nb, (t + 1) * nb)
        h = a[sl] * h + u[sl]
        o_ref[sl, :] = _bf16(h * yg_ref[sl, :])
    ht_ref[...] = h


def _lru_sample(xb, yg, conv_w, conv_b, gw16, gate_b, lam, h0, buf_tm, nb, nsteps):
    c = xb.shape[1]
    kconv = conv_w.shape[0]
    blk = gw16.shape[-1]
    cb = _tile(c, LRU_CB, blk)
    chan = lambda ci: (0, ci)
    return pl.pallas_call(
        functools.partial(_lru_sample_kernel, nb=nb, nsteps=nsteps, kconv=kconv),
        grid=(c // cb,),
        in_specs=[pl.BlockSpec((nsteps * nb, cb), chan),
                  pl.BlockSpec((nsteps * nb, cb), chan),
                  pl.BlockSpec((kconv, cb), chan),
                  pl.BlockSpec((1, cb), chan),
                  pl.BlockSpec((2, cb // blk, blk, blk), lambda ci: (0, ci, 0, 0)),
                  pl.BlockSpec((2, cb), chan),
                  pl.BlockSpec((1, cb), chan),
                  pl.BlockSpec((nb, cb), chan),
                  pl.BlockSpec(((kconv - 1) * nb, cb), chan)],
        out_shape=(jax.ShapeDtypeStruct((nsteps * nb, c), jnp.bfloat16),
                   jax.ShapeDtypeStruct((nb, c), jnp.float32)),
        out_specs=(pl.BlockSpec((nsteps * nb, cb), chan), pl.BlockSpec((nb, cb), chan)),
        compiler_params=_cparams(("parallel",)),
        name="lru_sample")(xb, yg, conv_w, conv_b, gw16, gate_b, lam, h0, buf_tm)


def _merge_kernel(a_ref, b_ref, w1_ref, w2_ref, g1_ref, g2_ref, o_ref):
    r1 = _dot(a_ref[...], w1_ref[...])
    r2 = _dot(b_ref[...], w2_ref[...])
    o_ref[...] = _bf16(g1_ref[...].astype(jnp.float32) * r1 + g2_ref[...].astype(jnp.float32) * r2)


def _merge(o_att16, o_lru16, wb16, g16):
    m, ka = o_att16.shape
    kb = o_lru16.shape[1]
    n = wb16.shape[1]
    tm = _tile(m, MERGE_TM, 16)
    tn = _tile(n, MERGE_TN, LANE)
    assert ka % kb == 0
    g2_off = n // tn
    return pl.pallas_call(
        _merge_kernel, grid=(m // tm, n // tn),
        in_specs=[pl.BlockSpec((tm, ka), lambda i, j: (i, 0)),
                  pl.BlockSpec((tm, kb), lambda i, j: (i, 0)),
                  pl.BlockSpec((ka, tn), lambda i, j: (0, j)),
                  pl.BlockSpec((kb, tn), lambda i, j: (ka // kb, j)),
                  pl.BlockSpec((tm, tn), lambda i, j: (i, j)),
                  pl.BlockSpec((tm, tn), lambda i, j: (i, j + g2_off))],
        out_shape=jax.ShapeDtypeStruct((m, n), jnp.bfloat16),
        out_specs=pl.BlockSpec((tm, tn), lambda i, j: (i, j)),
        compiler_params=_cparams(("parallel", "arbitrary")),
        name="merge")(o_att16, o_lru16, wb16, wb16, g16, g16)


def _mm_ln_kernel(a_ref, w_ref, res_ref, g_ref, b_ref, *rest, nk, alpha, want_bf16):
    o_ref = rest[0]
    k = pl.program_id(1)
    tm, n = o_ref.shape

    @pl.when(k == 0)
    def _():
        o_ref[...] = alpha * res_ref[...]

    for n0 in range(0, n, LN_NCHUNK):
        sl = slice(n0, min(n0 + LN_NCHUNK, n))
        o_ref[:, sl] += _dot(a_ref[...], w_ref[:, sl])

    @pl.when(k == nk - 1)
    def _():
        for r0 in range(0, tm, LN_RCHUNK):
            rs = slice(r0, min(r0 + LN_RCHUNK, tm))
            y = o_ref[rs, :]
            mu = jnp.mean(y, axis=-1, keepdims=True)
            yc = y - mu
            var = jnp.mean(yc * yc, axis=-1, keepdims=True)
            out = yc * lax.rsqrt(var + NORM_EPS) * g_ref[...] + b_ref[...]
            o_ref[rs, :] = out
            if want_bf16:
                rest[1][rs, :] = _bf16(out)


def _mm_ln(a16, w16, res, gain, bias, alpha, want_bf16):
    m, kdim = a16.shape
    n = w16.shape[1]
    tm = _tile(m, LN_TM, 16)
    fixed = tm * n * (2 * 4 + (2 * 2 if want_bf16 else 0) + 4) + 2 * tm * LN_NCHUNK * 4
    tk = _tile(kdim, LN_TK, LANE)
    while tk * 2 <= kdim and kdim % (tk * 2) == 0 and fixed + 2 * (tk * 2) * (n + tm) * 2 <= LN_VMEM_BUDGET:
        tk *= 2
    nk = kdim // tk
    row = pl.BlockSpec((tm, n), lambda i, k: (i, 0))
    vec = pl.BlockSpec((1, n), lambda i, k: (0, 0))
    out_shape = [jax.ShapeDtypeStruct((m, n), jnp.float32)]
    out_specs = [row]
    if want_bf16:
        out_shape.append(jax.ShapeDtypeStruct((m, n), jnp.bfloat16))
        out_specs.append(row)
    return pl.pallas_call(
        functools.partial(_mm_ln_kernel, nk=nk, alpha=alpha, want_bf16=want_bf16),
        grid=(m // tm, nk),
        in_specs=[pl.BlockSpec((tm, tk), lambda i, k: (i, k)),
                  pl.BlockSpec((tk, n), lambda i, k: (k, 0)),
                  pl.BlockSpec((tm, n), lambda i, k: (i, 0), pipeline_mode=pl.Buffered(1)),
                  vec, vec],
        out_shape=tuple(out_shape), out_specs=tuple(out_specs),
        compiler_params=_cparams(("parallel", "arbitrary")),
        name="mm_ln")(a16, w16, res, gain, bias)


def _ffn_up_prompt_kernel(x_ref, wu_ref, wp_ref, cw_ref, cb_ref, buf_ref, o_ref, tail_ref, ext_sc, *,
                          tm, tiles_per_seq, kconv):
    i = pl.program_id(1)
    pad = SUBLANE
    nprev = kconv - 1

    @pl.when(i % tiles_per_seq == 0)
    def _():
        ext_sc[pad - nprev:pad, :] = buf_ref[0]

    x = x_ref[...]
    u = _dot(x, wu_ref[...])
    up = _dot(x, wp_ref[...])
    ext_sc[pad:pad + tm, :] = u
    uc = cb_ref[...] + cw_ref[nprev:kconv, :] * u
    for s in range(1, kconv):
        uc = uc + cw_ref[nprev - s:kconv - s, :] * ext_sc[pad - s:pad - s + tm, :]
    tail = ext_sc[pad + tm - nprev:pad + tm, :]
    ext_sc[pad - nprev:pad, :] = tail
    tail_ref[0] = ext_sc[tm:tm + pad, :]
    o_ref[...] = _bf16(jax.nn.silu(uc) * up)


def _ffn_up_prompt(y16, wu16, wp16, conv_w, conv_b, buf0, batch, seq):
    m, kdim = y16.shape
    n = wu16.shape[1]
    kconv = conv_w.shape[0]
    tm = _tile(seq, FFN_TM, 16)
    tn = _tile(n, FFN_TN, LANE)
    tps = seq // tm
    col = lambda j, i: (0, j)
    return pl.pallas_call(
        functools.partial(_ffn_up_prompt_kernel, tm=tm, tiles_per_seq=tps, kconv=kconv),
        grid=(n // tn, m // tm),
        in_specs=[pl.BlockSpec((tm, kdim), lambda j, i: (i, 0)),
                  pl.BlockSpec((kdim, tn), col),
                  pl.BlockSpec((kdim, tn), col),
                  pl.BlockSpec((kconv, tn), col),
                  pl.BlockSpec((1, tn), col),
                  pl.BlockSpec((1, kconv - 1, tn), lambda j, i: (i // tps, 0, j))],
        out_shape=(jax.ShapeDtypeStruct((m, n), jnp.bfloat16),
                   jax.ShapeDtypeStruct((batch, SUBLANE, n), jnp.float32)),
        out_specs=(pl.BlockSpec((tm, tn), lambda j, i: (i, j)),
                   pl.BlockSpec((1, SUBLANE, tn), lambda j, i: (i // tps, 0, j))),
        scratch_shapes=[pltpu.VMEM((tm + SUBLANE, tn), jnp.float32)],
        compiler_params=_cparams(("parallel", "arbitrary")),
        name="ffn_up_prompt")(y16, wu16, wp16, conv_w, conv_b, buf0)


def _ffn_up_sample_kernel(x_ref, wu_ref, wp_ref, cw_ref, cb_ref, buf_ref, o_ref, tail_ref, *, nb, nsteps, kconv):
    x = x_ref[...]
    u = _dot(x, wu_ref[...])
    up = _dot(x, wp_ref[...])
    full = jnp.concatenate([buf_ref[...], u], axis=0)
    uc = cb_ref[...] + cw_ref[0:1, :] * full[0:nsteps * nb]
    for j in range(1, kconv):
        uc = uc + cw_ref[j:j + 1, :] * full[j * nb:(j + nsteps) * nb]
    tail_ref[...] = full[nsteps * nb:(nsteps + kconv - 1) * nb]
    o_ref[...] = _bf16(jax.nn.silu(uc) * up)


def _ffn_up_sample(y16, wu16, wp16, conv_w, conv_b, buf_tm, nb, nsteps):
    m, kdim = y16.shape
    n = wu16.shape[1]
    kconv = conv_w.shape[0]
    tn = _tile(n, FFN_TN, LANE)
    col = lambda j: (0, j)
    return pl.pallas_call(
        functools.partial(_ffn_up_sample_kernel, nb=nb, nsteps=nsteps, kconv=kconv),
        grid=(n // tn,),
        in_specs=[pl.BlockSpec((m, kdim), lambda j: (0, 0)),
                  pl.BlockSpec((kdim, tn), col),
                  pl.BlockSpec((kdim, tn), col),
                  pl.BlockSpec((kconv, tn), col),
                  pl.BlockSpec((1, tn), col),
                  pl.BlockSpec(((kconv - 1) * nb, tn), col)],
        out_shape=(jax.ShapeDtypeStruct((m, n), jnp.bfloat16),
                   jax.ShapeDtypeStruct(((kconv - 1) * nb, n), jnp.float32)),
        out_specs=(pl.BlockSpec((m, tn), col), pl.BlockSpec(((kconv - 1) * nb, tn), col)),
        compiler_params=_cparams(("parallel",)),
        name="ffn_up_sample")(y16, wu16, wp16, conv_w, conv_b, buf_tm)


def _pad_cols(a, n):
    return jnp.pad(a, [(0, 0)] * (a.ndim - 1) + [(0, n - a.shape[-1])])


def _cast_pad_kernel(x_ref, o_ref, *, nrb, ncb):
    valid = (pl.program_id(0) < nrb) & (pl.program_id(1) < ncb)

    @pl.when(valid)
    def _():
        o_ref[...] = _bf16(x_ref[0])

    @pl.when(jnp.logical_not(valid))
    def _():
        o_ref[...] = jnp.zeros(o_ref.shape, o_ref.dtype)


def _cast_pad(w, layer, col_off, width, rows_out, cols_out):
    rows = w.shape[1]
    cb = _tile(math.gcd(math.gcd(width, cols_out), col_off) if col_off else math.gcd(width, cols_out), 4096, LANE)
    rb = _tile(math.gcd(rows, rows_out), max(CAST_BLOCK_ELEMS // cb, 16), 16)
    nrb, ncb, cob = rows // rb, width // cb, col_off // cb
    return pl.pallas_call(
        functools.partial(_cast_pad_kernel, nrb=nrb, ncb=ncb),
        grid=(rows_out // rb, cols_out // cb),
        in_specs=[pl.BlockSpec((1, rb, cb),
                               lambda i, j: (layer, jnp.minimum(i, nrb - 1), jnp.minimum(j, ncb - 1) + cob))],
        out_shape=jax.ShapeDtypeStruct((rows_out, cols_out), jnp.bfloat16),
        out_specs=pl.BlockSpec((rb, cb), lambda i, j: (i, j)),
        compiler_params=_cparams(("parallel", "parallel")),
        name="cast_pad")(w)


def kernel(x_prompt, x_sample, cache_k, cache_v, page_table, state_lru_h, state_lru_conv, state_ffn_conv,
           w_in, diff_lambda, subln_w, lru_conv_w, lru_conv_b, lru_gate_w, lru_gate_b, lru_lambda,
           w_branch, w_out, ln1_g, ln1_b, ffn_w_in, ffn_conv_w, ffn_conv_b, ffn_w_down, ln2_g, ln2_b):
    depth, d_model, in_w = w_in.shape
    bp, tp, _ = x_prompt.shape
    bs, ts, _ = x_sample.shape
    n_pool, page, kvh, _, hd = cache_k.shape[1:]
    vd = cache_v.shape[-1]
    lru_w = lru_lambda.shape[-1]
    lru_heads, lru_blk = lru_gate_w.shape[2], lru_gate_w.shape[3]
    d_ff = ffn_conv_w.shape[-1]
    k_w, v_w = kvh * 2 * hd, kvh * vd
    q_w = in_w - k_w - v_w - 2 * lru_w - 2 * d_model
    n_heads = q_w // (2 * hd)
    group = n_heads // kvh
    att_out_w = n_heads * vd
    rot_dim = hd // 4
    past = page_table.shape[1] * page
    alpha = (2.0 * depth) ** 0.25
    o_k, o_v, o_xb, o_yb, o_g = q_w, q_w + k_w, q_w + k_w + v_w, q_w + k_w + v_w + lru_w, q_w + k_w + v_w + 2 * lru_w
    ff_align = max(FFN_TN, LN_TK)
    d_ff_pad = -(-d_ff // ff_align) * ff_align

    mp, ms = bp * tp, bs * ts
    tab_rows = _tile(tp, PROJ_TM, 16)
    tab_p = _rope_tables(tp // tab_rows, tab_rows, rot_dim, 0, True)
    tab_s = _rope_tables(ts, bs, rot_dim, past, False)

    yp = x_prompt.reshape(mp, d_model)
    ys = x_sample.transpose(1, 0, 2).reshape(ms, d_model)
    outs_p, outs_s = [], []
    for l in range(depth):
        lam_init = 0.8 - 0.6 * math.exp(-0.3 * l)
        w_in16 = _bf16(w_in[l])
        wb16 = _bf16(w_branch[l])
        wo16 = _bf16(w_out[l])
        wu16 = _cast_pad(ffn_w_in, l, 0, d_ff, d_model, d_ff_pad)
        wp16 = _cast_pad(ffn_w_in, l, d_ff, d_ff, d_model, d_ff_pad)
        wd16 = _cast_pad(ffn_w_down, l, 0, d_model, d_ff_pad, d_model)
        fcw = _pad_cols(ffn_conv_w[l], d_ff_pad)
        fcb = _pad_cols(ffn_conv_b[l][None], d_ff_pad)
        gw16 = _bf16(lru_gate_w[l])
        gate_b = lru_gate_b[l].reshape(2, lru_w)
        dl = diff_lambda[l]
        sub_w = subln_w[l][None]
        common = dict(conv_w=lru_conv_w[l], conv_b=lru_conv_b[l][None], gw16=gw16, gate_b=gate_b,
                      lam=lru_lambda[l][None])

        def project(x16, tables):
            q16, = _proj(x16, w_in16, 0, q_w, "rope", False, True, tables, rot_dim // 2, out_scale=hd ** -0.5)
            k32, k16 = _proj(x16, w_in16, o_k, k_w, "rope", True, True, tables, rot_dim // 2)
            v32, v16 = _proj(x16, w_in16, o_v, v_w, "plain", True, True)
            xb, = _proj(x16, w_in16, o_xb, lru_w, "plain", True, False)
            yg, = _proj(x16, w_in16, o_yb, lru_w, "gelu", True, False)
            g16, = _proj(x16, w_in16, o_g, 2 * d_model, "sigmoid", False, True)
            return q16, k32, k16, v32, v16, xb, yg, g16

        def finish(x32, o_att16, o_lru16, g16, ffn_up):
            merged16 = _merge(o_att16, o_lru16, wb16, g16)
            y1, y1_16 = _mm_ln(merged16, wo16, x32, ln1_g[l][None], ln1_b[l][None], alpha, True)
            f16, tail = ffn_up(y1_16)
            y2, = _mm_ln(f16, wd16, y1, ln2_g[l][None], ln2_b[l][None], alpha, False)
            return y2, tail

        q16, k32, k16, v32, v16, xb, yg, g16 = project(_bf16(yp), tab_p)
        o_att16 = _attn_prompt(q16, k16, v16, dl, sub_w, bp, tp, kvh, group, hd, vd, lam_init)
        zeros = lambda *s: jnp.zeros(s, jnp.float32)
        o_lru16, ht = _lru_prompt(xb, yg, h0=zeros(bp, 1, lru_w), buf0=zeros(bp, lru_conv_w.shape[1] - 1, lru_w),
                                  batch=bp, seq=tp, **common)
        kf = ffn_conv_w.shape[1]
        yp, tail = finish(yp, o_att16, o_lru16, g16,
                          lambda a: _ffn_up_prompt(a, wu16, wp16, fcw, fcb, zeros(bp, kf - 1, d_ff_pad), bp, tp))
        kl = lru_conv_w.shape[1]
        outs_p.append((k32.reshape(bp, tp, kvh, 2, hd), v32.reshape(bp, tp, kvh, vd), ht.reshape(bp, lru_w),
                       xb.reshape(bp, tp, lru_w)[:, tp - (kl - 1):],
                       tail[:, SUBLANE - (kf - 1):, :d_ff]))

        q16, k32, k16, v32, v16, xb, yg, g16 = project(_bf16(ys), tab_s)
        qs = q16.reshape(ts, bs, kvh, group, 2, hd).transpose(1, 4, 2, 3, 0, 5).reshape(bs, 2, kvh * group * ts, hd)
        new_rows = -(-ts * kvh // LANE) * LANE
        kn = k16.reshape(ts, bs, kvh, 2, hd).transpose(1, 3, 0, 2, 4).reshape(bs, 2, ts * kvh, hd)
        kn = jnp.pad(kn, ((0, 0), (0, 0), (0, new_rows - ts * kvh), (0, 0)))
        vn = v16.reshape(ts, bs, kvh, vd).transpose(1, 0, 2, 3).reshape(bs, ts * kvh, vd)
        vn = jnp.pad(vn, ((0, 0), (0, new_rows - ts * kvh), (0, 0)))
        o_s = _attn_sample(qs, kn, vn, cache_k, cache_v, l, page_table, dl, sub_w, group, ts, lam_init)
        o_att16 = _bf16(o_s.reshape(bs, kvh, group, ts, vd).transpose(3, 0, 1, 2, 4).reshape(ms, att_out_w))
        to_tm = lambda a: a.transpose(1, 0, 2).reshape(-1, a.shape[-1])
        o_lru16, ht = _lru_sample(xb, yg, h0=state_lru_h[l], buf_tm=to_tm(state_lru_conv[l]), nb=bs, nsteps=ts,
                                  **common)
        fbuf = _pad_cols(to_tm(state_ffn_conv[l]), d_ff_pad)
        ys, tail = finish(ys, o_att16, o_lru16, g16,
                          lambda a: _ffn_up_sample(a, wu16, wp16, fcw, fcb, fbuf, bs, ts))
        from_tm = lambda a, n: a.reshape(n, bs, -1).transpose(1, 0, 2)
        lru_full = jnp.concatenate([to_tm(state_lru_conv[l]), xb], axis=0)
        outs_s.append((from_tm(k32, ts).reshape(bs, ts, kvh, 2, hd), from_tm(v32, ts).reshape(bs, ts, kvh, vd), ht,
                       from_tm(lru_full[lru_full.shape[0] - (kl - 1) * bs:], kl - 1),
                       from_tm(tail, kf - 1)[:, :, :d_ff]))

    y_prompt = yp.reshape(bp, tp, d_model)
    y_sample = ys.reshape(ts, bs, d_model).transpose(1, 0, 2)
    stack = lambda outs, i: jnp.stack([o[i] for o in outs])
    return (y_prompt, y_sample,
            stack(outs_p, 0), stack(outs_p, 1), stack(outs_p, 2), stack(outs_p, 3), stack(outs_p, 4),
            stack(outs_s, 0), stack(outs_s, 1), stack(outs_s, 2), stack(outs_s, 3), stack(outs_s, 4))
```
